```python
import jax, jax.numpy as jnp
from jax import lax
import numpy as np

D_MODEL = 1024
BATCH = 16
SEQ = 256
DEPTH = 4
DEC_BATCH = 4
DEC_SEQ = 1024
PAST_LEN = 256

GRID_W = 64
POS_THETA = 10000.0
N_MOD = 6
RMS_EPS = 1e-6

DN_H = 4
DN_DK = 128
DN_DV = 128
DN_W = DN_H * DN_DV
DN_QKV = 2 * DN_H * DN_DK + DN_W
CONV_K = 5
DN_CHUNK = 64

RW_H = 4
RW_DH = 64
RW_W = RW_H * RW_DH
RW_W_LORA = 64
RW_A_LORA = 64
RW_G_LORA = 128
RW_GN_EPS = 64e-5

GLA_H = 4
GLA_DK = 32
GLA_DV = 64
GLA_KW = GLA_H * GLA_DK
GLA_VW = GLA_H * GLA_DV
GLA_GK_LORA = 16
GLA_GATE_NORM = 16.0
GLA_CHUNK = 16

MIX_W = DN_W + RW_W + GLA_VW

DN_SIZES = (DN_H * DN_DK, DN_H * DN_DK, DN_W, DN_W, 2 * DN_H, 2 * DN_H)
RW_SIZES = (RW_W, RW_W, RW_W, 2 * RW_W_LORA, 2 * RW_A_LORA, RW_G_LORA)
GLA_SIZES = (GLA_KW, GLA_KW, GLA_VW, GLA_VW, 2 * GLA_GK_LORA)
DN_COLS = sum(DN_SIZES)
RW_COLS = sum(RW_SIZES)
GLA_COLS = sum(GLA_SIZES)
IN_COLS = DN_COLS + RW_COLS + GLA_COLS

N_EXPERTS = 32
TOP_K = 4
D_FF = 1024
SWIGLU_LIMIT = 7.0
SWIGLU_ALPHA = 1.702
MOE_BLOCK = 128

kernel_name = 'hybrid_flow_trunk_step'


def rmsnorm(x, g, eps=RMS_EPS):
    xf = x.astype(jnp.float32)
    y = xf * lax.rsqrt(jnp.mean(xf * xf, axis=-1, keepdims=True) + eps)
    return (y * g.astype(jnp.float32)).astype(x.dtype)


def l2norm(x, eps=1e-6):
    return x * lax.rsqrt(jnp.sum(x * x, axis=-1, keepdims=True) + eps)


def split_cols(u, sizes):
    idx = [int(i) for i in np.cumsum(sizes)[:-1]]
    return jnp.split(u, idx, axis=-1)


def centred_dwconv(u, w):
    return lax.conv_general_dilated(u, w[:, None, :].astype(u.dtype), window_strides=(1,), padding='SAME',
                                    dimension_numbers=('NWC', 'WIO', 'NWC'), feature_group_count=u.shape[-1])


def centred_shift(u):
    z = jnp.zeros_like(u[:, :1])
    return 0.5 * (jnp.concatenate([z, u[:, :-1]], axis=1) + jnp.concatenate([u[:, 1:], z], axis=1))


def grid_pos_embed(rows, dim):
    t = jnp.arange(rows * GRID_W)
    r = (t // GRID_W).astype(jnp.float32)
    col = (t % GRID_W).astype(jnp.float32)
    quarter = dim // 4
    omega = 1.0 / (POS_THETA ** (jnp.arange(quarter, dtype=jnp.float32) / quarter))
    er = r[:, None] * omega
    ec = col[:, None] * omega
    return jnp.concatenate([jnp.sin(er), jnp.cos(er), jnp.sin(ec), jnp.cos(ec)], axis=-1)


def gated_delta_chunked(q, k, v, beta, g, s0):
    b, t, h, dk = q.shape
    dv = v.shape[-1]
    c = DN_CHUNK
    n = t // c

    def to_chunks(a):
        return a.reshape(b, n, c, h, *a.shape[3:]).swapaxes(2, 3)

    q, k, v, beta, g = (to_chunks(a) for a in (q, k, v, beta, g))
    G = jnp.cumsum(g, axis=-1)
    tri_incl = jnp.tril(jnp.ones((c, c), bool))
    tri_strict = jnp.tril(jnp.ones((c, c), bool), -1)
    decay = jnp.exp(jnp.where(tri_incl, G[..., :, None] - G[..., None, :], -jnp.inf))
    kk = jnp.einsum('bnhid,bnhjd->bnhij', k, k)
    lower = jnp.where(tri_strict, beta[..., :, None] * kk * decay, 0.0)
    eye_l = jnp.eye(c, dtype=lower.dtype) + lower
    rhs = jnp.concatenate([beta[..., None] * v, (beta * jnp.exp(G))[..., None] * k], axis=-1)
    sol = lax.linalg.triangular_solve(eye_l, rhs, left_side=True, lower=True, unit_diagonal=True)
    u_tilde, w_k = sol[..., :dv], sol[..., dv:]
    qk = jnp.einsum('bnhid,bnhjd->bnhij', q, k) * decay
    q_g = q * jnp.exp(G)[..., None]
    k_g = k * jnp.exp(G[..., -1:] - G)[..., None]
    g_last = jnp.exp(G[..., -1])

    def step(s, xs):
        ut, wk, qkc, qg, kg, gl = xs
        u = ut - jnp.einsum('bhcd,bhde->bhce', wk, s)
        o = jnp.einsum('bhcd,bhde->bhce', qg, s) + jnp.einsum('bhij,bhje->bhie', qkc, u)
        s = gl[..., None, None] * s + jnp.einsum('bhcd,bhce->bhde', kg, u)
        return s, o

    xs = tuple(a.swapaxes(0, 1) for a in (u_tilde, w_k, qk, q_g, k_g, g_last))
    s_fin, o = lax.scan(step, s0, xs)
    o = o.swapaxes(0, 1).swapaxes(2, 3).reshape(b, t, h, dv)
    return o, s_fin


def gla_chunked(q, k, v, gk, s0):
    b, t, h, dk = q.shape
    dv = v.shape[-1]
    c = GLA_CHUNK
    n = t // c

    def to_chunks(a):
        return a.reshape(b, n, c, h, a.shape[-1]).swapaxes(2, 3)

    q, k, v, gk = (to_chunks(a) for a in (q, k, v, gk))
    bc = jnp.cumsum(gk, axis=3)
    tri = jnp.tril(jnp.ones((c, c), bool))[:, :, None]
    dec = jnp.exp(jnp.where(tri, bc[..., :, None, :] - bc[..., None, :, :], -jnp.inf))
    att = jnp.einsum('bnhid,bnhjd,bnhijd->bnhij', q, k, dec)
    o_intra = jnp.einsum('bnhij,bnhjv->bnhiv', att, v)
    q_g = q * jnp.exp(bc)
    k_g = k * jnp.exp(bc[..., -1:, :] - bc)
    g_last = jnp.exp(bc[..., -1, :])

    def step(s, xs):
        kg, vv, gl = xs
        return gl[..., None] * s + jnp.einsum('bhcd,bhcv->bhdv', kg, vv), s

    s_fin, s_start = lax.scan(step, s0, (k_g.swapaxes(0, 1), v.swapaxes(0, 1), g_last.swapaxes(0, 1)))
    o = jnp.einsum('bnhcd,bnhdv->bnhcv', q_g, s_start.swapaxes(0, 1)) + o_intra
    o = o.swapaxes(2, 3).reshape(b, t, h, dv)
    return o, s_fin


def rwkv7_scan(r, w, k, v, a, bb, s0, reverse):
    def step(s, xs):
        rt, wt, kt, vt, at, bt = xs
        sa = jnp.einsum('bhvk,bhk->bhv', s, at)
        s = s * wt[:, :, None, :] + sa[..., None] * bt[:, :, None, :] + vt[..., None] * kt[:, :, None, :]
        return s, jnp.einsum('bhvk,bhk->bhv', s, rt)

    xs = tuple(x.swapaxes(0, 1) for x in (r, w, k, v, a, bb))
    s_fin, y = lax.scan(step, s0, xs, reverse=reverse)
    return y.swapaxes(0, 1), s_fin


def deltanet_mixer(parts, conv_w, a_log, dt_bias, norm_w, s0_f, s0_b):
    q, k, v, z, bt, at = parts
    b, t, _ = q.shape
    qkv = jax.nn.silu(centred_dwconv(jnp.concatenate([q, k, v], axis=-1), conv_w))
    q, k, v = jnp.split(qkv, [DN_H * DN_DK, 2 * DN_H * DN_DK], axis=-1)
    q = l2norm(q.reshape(b, t, DN_H, DN_DK)) * DN_DK ** -0.5
    k = l2norm(k.reshape(b, t, DN_H, DN_DK))
    v = v.reshape(b, t, DN_H, DN_DV)
    beta = jax.nn.sigmoid(bt).reshape(b, t, 2, DN_H)
    g = -jnp.exp(a_log) * jax.nn.softplus(at.reshape(b, t, 2, DN_H) + dt_bias)
    o_f, s_f = gated_delta_chunked(q, k, v, beta[:, :, 0], g[:, :, 0], s0_f)
    fl = lambda a: jnp.flip(a, axis=1)
    o_b, s_b = gated_delta_chunked(fl(q), fl(k), fl(v), fl(beta[:, :, 1]), fl(g[:, :, 1]), s0_b)
    o = rmsnorm(o_f + fl(o_b), norm_w) * jax.nn.silu(z.reshape(b, t, DN_H, DN_DV))
    return o.reshape(b, t, DN_W), s_f, s_b


def rwkv7_mixer(parts, w0, w2, a0, a2, g2, k_k, k_a, r_k, ln_w, ln_b, s0_f, s0_b):
    r, k, v, wd, ad, gd = parts
    b, t, _ = r.shape
    wd = wd.reshape(b, t, 2, RW_W_LORA)
    ad = ad.reshape(b, t, 2, RW_A_LORA)
    w_log = -jax.nn.softplus(-(w0 + jnp.einsum('btdr,drc->btdc', jnp.tanh(wd), w2))) - 0.5
    decay = jnp.exp(-jnp.exp(w_log))
    a = jax.nn.sigmoid(a0 + jnp.einsum('btdr,drc->btdc', ad, a2))
    gate = jax.nn.sigmoid(gd) @ g2
    heads = lambda x: x.reshape(*x.shape[:-1], RW_H, RW_DH)
    kk = l2norm(heads(k * k_k))
    k_dir = k[:, :, None, :] * (1.0 + (a - 1.0) * k_a)
    r_h, k_h, v_h = heads(r), heads(k), heads(v)
    y_f, s_f = rwkv7_scan(r_h, heads(decay[:, :, 0]), heads(k_dir[:, :, 0]), v_h, -kk,
                          kk * heads(a[:, :, 0]), s0_f, False)
    y_b, s_b = rwkv7_scan(r_h, heads(decay[:, :, 1]), heads(k_dir[:, :, 1]), v_h, -kk,
                          kk * heads(a[:, :, 1]), s0_b, True)
    y = y_f + y_b
    mu = jnp.mean(y, axis=-1, keepdims=True)
    var = jnp.mean(jnp.square(y - mu), axis=-1, keepdims=True)
    yn = ((y - mu) * lax.rsqrt(var + RW_GN_EPS)).reshape(b, t, RW_W) * ln_w + ln_b
    bonus = (jnp.sum(r_h * k_h * r_k, axis=-1, keepdims=True) * v_h).reshape(b, t, RW_W)
    return (yn + bonus) * gate, s_f, s_b


def gla_mixer(parts, gk2, gk_b, norm_w, s0_f, s0_b):
    q, k, v, g, gkd = parts
    b, t, _ = q.shape
    q = q.reshape(b, t, GLA_H, GLA_DK) * GLA_DK ** -0.5
    k = k.reshape(b, t, GLA_H, GLA_DK)
    v = v.reshape(b, t, GLA_H, GLA_DV)
    gkd = gkd.reshape(b, t, 2, GLA_GK_LORA)
    gk = jax.nn.log_sigmoid(jnp.einsum('btdr,drc->btdc', gkd, gk2) + gk_b) / GLA_GATE_NORM
    gk = gk.reshape(b, t, 2, GLA_H, GLA_DK)
    o_f, s_f = gla_chunked(q, k, v, gk[:, :, 0], s0_f)
    fl = lambda a: jnp.flip(a, axis=1)
    o_b, s_b = gla_chunked(fl(q), fl(k), fl(v), fl(gk[:, :, 1]), s0_b)
    o = rmsnorm(o_f + fl(o_b), norm_w) * jax.nn.silu(g.reshape(b, t, GLA_H, GLA_DV))
    return o.reshape(b, t, GLA_VW), s_f, s_b


def moe_ffn(h, router_w, router_b, w_gu, b_gu, w_dn, b_dn):
    b, t, d = h.shape
    x = h.reshape(b * t, d)
    n_tok = b * t
    logits = (x @ router_w).astype(jnp.float32) + router_b.astype(jnp.float32)
    top_val, top_idx = lax.top_k(logits, TOP_K)
    gates = jax.nn.softmax(top_val, axis=-1)
    n_assign = n_tok * TOP_K
    e_flat = top_idx.reshape(-1)
    order = jnp.argsort(e_flat)
    e_sorted = e_flat[order]
    tok_sorted = order // TOP_K
    counts = jnp.bincount(e_flat, length=N_EXPERTS)
    padded = (counts + MOE_BLOCK - 1) // MOE_BLOCK * MOE_BLOCK
    pad_end = jnp.cumsum(padded)
    pad_start = pad_end - padded
    start = jnp.cumsum(counts) - counts
    dest = pad_start[e_sorted] + jnp.arange(n_assign) - start[e_sorted]
    n_blocks = -(-n_assign // MOE_BLOCK) + N_EXPERTS
    n_rows = n_blocks * MOE_BLOCK
    buf = jnp.zeros((n_rows, d), x.dtype).at[dest].set(x[tok_sorted])
    block_e = jnp.minimum(jnp.searchsorted(pad_end, jnp.arange(n_blocks) * MOE_BLOCK, side='right'), N_EXPERTS - 1)

    def expert_block(args):
        xb, e = args
        gu = xb @ w_gu[e] + b_gu[e]
        gt = jnp.minimum(gu[:, :D_FF], SWIGLU_LIMIT)
        up = jnp.clip(gu[:, D_FF:], -SWIGLU_LIMIT, SWIGLU_LIMIT)
        glu = gt * jax.nn.sigmoid(gt * SWIGLU_ALPHA)
        return ((up + 1.0) * glu) @ w_dn[e] + b_dn[e]

    y_buf = lax.map(expert_block, (buf.reshape(n_blocks, MOE_BLOCK, d), block_e))
    y_rows = y_buf.reshape(n_rows, d)[dest]
    wts = gates.reshape(-1)[order].astype(x.dtype)
    out = jnp.zeros((n_tok, d), x.dtype).at[tok_sorted].add(y_rows * wts[:, None])
    return out.reshape(b, t, d)


def trunk_layer(x, cond, s_dn, s_rw, s_gla, w_mod, b_mod, norm_mix, norm_ffn, w_in, w_out,
                dn_conv, dn_a_log, dn_dt_bias, dn_norm,
                rw_mu, rw_w0, rw_w2, rw_a0, rw_a2, rw_g2, rw_k_k, rw_k_a, rw_r_k, rw_ln_w, rw_ln_b,
                gla_gk2, gla_gk_b, gla_norm,
                router_w, router_b, moe_w_gu, moe_b_gu, moe_w_dn, moe_b_dn):
    mod = (jax.nn.silu(cond) @ w_mod + b_mod)[:, None, :]
    sh_a, sc_a, g_a, sh_f, sc_f, g_f = jnp.split(mod, N_MOD, axis=-1)
    h = rmsnorm(x, norm_mix) * (1 + sc_a) + sh_a
    u = (h @ w_in).astype(jnp.float32)
    dn_u, rw_u, gla_u = split_cols(u, (DN_COLS, RW_COLS, GLA_COLS))
    rw_u = rw_u + rw_mu * (centred_shift(rw_u) - rw_u)
    s_dn = s_dn.astype(jnp.float32)
    s_rw = s_rw.astype(jnp.float32)
    s_gla = s_gla.astype(jnp.float32)
    o_dn, dn_f, dn_b = deltanet_mixer(split_cols(dn_u, DN_SIZES), dn_conv, dn_a_log, dn_dt_bias, dn_norm,
                                      s_dn[:, 0], s_dn[:, 1])
    o_rw, rw_f, rw_b = rwkv7_mixer(split_cols(rw_u, RW_SIZES), rw_w0, rw_w2, rw_a0, rw_a2, rw_g2,
                                   rw_k_k, rw_k_a, rw_r_k, rw_ln_w, rw_ln_b, s_rw[:, 0], s_rw[:, 1])
    o_gla, gla_f, gla_b = gla_mixer(split_cols(gla_u, GLA_SIZES), gla_gk2, gla_gk_b, gla_norm,
                                    s_gla[:, 0], s_gla[:, 1])
    o = jnp.concatenate([o_dn, o_rw, o_gla], axis=-1).astype(x.dtype) @ w_out
    x = x + g_a * o
    h = rmsnorm(x, norm_ffn) * (1 + sc_f) + sh_f
    x = x + g_f * moe_ffn(h, router_w, router_b, moe_w_gu, moe_b_gu, moe_w_dn, moe_b_dn)
    return x, jnp.stack([dn_f, dn_b], axis=1), jnp.stack([rw_f, rw_b], axis=1), jnp.stack([gla_f, gla_b], axis=1)


def setup_inputs(seed: int = 0) -> dict:
    key = jax.random.key(seed)
    kit = iter(jax.random.split(key, 48))

    def nrm(shape, scale):
        return jax.random.normal(next(kit), shape, jnp.float32) * scale

    def unif(shape, lo, hi):
        return jax.random.uniform(next(kit), shape, jnp.float32, lo, hi)

    dt = jnp.exp(unif((DEPTH, 2, DN_H), float(np.log(1e-3)), float(np.log(1e-1))))
    return {
        'x_prompt': nrm((BATCH, SEQ, D_MODEL), 1.0),
        'x_sample': nrm((DEC_BATCH, DEC_SEQ, D_MODEL), 1.0),
        'state_delta': nrm((DEC_BATCH, DEPTH, 2, DN_H, DN_DK, DN_DV), 0.3),
        'state_rwkv': nrm((DEC_BATCH, DEPTH, 2, RW_H, RW_DH, RW_DH), 0.3),
        'state_gla': nrm((DEC_BATCH, DEPTH, 2, GLA_H, GLA_DK, GLA_DV), 0.3),
        'c': nrm((DEC_BATCH, D_MODEL), 1.0),
        'c_ctx': nrm((D_MODEL,), 1.0),
        'w_mod': nrm((DEPTH, D_MODEL, N_MOD * D_MODEL), 0.5 * D_MODEL ** -0.5),
        'b_mod': nrm((DEPTH, N_MOD * D_MODEL), 0.02),
        'norm_mix': 1.0 + nrm((DEPTH, D_MODEL), 0.02),
        'norm_ffn': 1.0 + nrm((DEPTH, D_MODEL), 0.02),
        'norm_out': 1.0 + nrm((D_MODEL,), 0.02),
        'w_in': nrm((DEPTH, D_MODEL, IN_COLS), D_MODEL ** -0.5),
        'w_out': nrm((DEPTH, MIX_W, D_MODEL), MIX_W ** -0.5),
        'dn_conv': nrm((DEPTH, CONV_K, DN_QKV), CONV_K ** -0.5),
        'dn_a_log': jnp.log(unif((DEPTH, 2, DN_H), 1.0, 16.0)),
        'dn_dt_bias': jnp.log(jnp.expm1(dt)),
        'dn_norm': 1.0 + nrm((DEPTH, DN_DV), 0.02),
        'rw_mu': unif((DEPTH, RW_COLS), 0.0, 1.0),
        'rw_w0': unif((DEPTH, 2, RW_W), -5.0, 0.0),
        'rw_w2': nrm((DEPTH, 2, RW_W_LORA, RW_W), 0.5 * RW_W_LORA ** -0.5),
        'rw_a0': nrm((DEPTH, 2, RW_W), 0.1),
        'rw_a2': nrm((DEPTH, 2, RW_A_LORA, RW_W), RW_A_LORA ** -0.5),
        'rw_g2': nrm((DEPTH, RW_G_LORA, RW_W), RW_G_LORA ** -0.5),
        'rw_k_k': 0.85 + nrm((DEPTH, RW_W), 0.02),
        'rw_k_a': 1.0 + nrm((DEPTH, RW_W), 0.02),
        'rw_r_k': nrm((DEPTH, RW_H, RW_DH), 0.1),
        'rw_ln_w': 1.0 + nrm((DEPTH, RW_W), 0.02),
        'rw_ln_b': nrm((DEPTH, RW_W), 0.02),
        'gla_gk2': nrm((DEPTH, 2, GLA_GK_LORA, GLA_KW), GLA_GK_LORA ** -0.5),
        'gla_gk_b': nrm((DEPTH, 2, GLA_KW), 0.1),
        'gla_norm': 1.0 + nrm((DEPTH, GLA_DV), 0.02),
        'router_w': nrm((DEPTH, D_MODEL, N_EXPERTS), D_MODEL ** -0.5),
        'router_b': nrm((DEPTH, N_EXPERTS), 0.01),
        'moe_w_gu': nrm((DEPTH, N_EXPERTS, D_MODEL, 2 * D_FF), D_MODEL ** -0.5),
        'moe_b_gu': nrm((DEPTH, N_EXPERTS, 2 * D_FF), 0.01),
        'moe_w_dn': nrm((DEPTH, N_EXPERTS, D_FF, D_MODEL), D_FF ** -0.5),
        'moe_b_dn': nrm((DEPTH, N_EXPERTS, D_MODEL), 0.01),
    }


def reference(x_prompt, x_sample, state_delta, state_rwkv, state_gla, c, c_ctx,
              w_mod, b_mod, norm_mix, norm_ffn, norm_out, w_in, w_out,
              dn_conv, dn_a_log, dn_dt_bias, dn_norm,
              rw_mu, rw_w0, rw_w2, rw_a0, rw_a2, rw_g2, rw_k_k, rw_k_a, rw_r_k, rw_ln_w, rw_ln_b,
              gla_gk2, gla_gk_b, gla_norm,
              router_w, router_b, moe_w_gu, moe_b_gu, moe_w_dn, moe_b_dn):
    def run_layer(x, cond, s_dn, s_rw, s_gla, l):
        return trunk_layer(x, cond, s_dn, s_rw, s_gla, w_mod[l], b_mod[l], norm_mix[l], norm_ffn[l],
                           w_in[l], w_out[l], dn_conv[l], dn_a_log[l], dn_dt_bias[l], dn_norm[l],
                           rw_mu[l], rw_w0[l], rw_w2[l], rw_a0[l], rw_a2[l], rw_g2[l], rw_k_k[l], rw_k_a[l],
                           rw_r_k[l], rw_ln_w[l], rw_ln_b[l], gla_gk2[l], gla_gk_b[l], gla_norm[l],
                           router_w[l], router_b[l], moe_w_gu[l], moe_b_gu[l], moe_w_dn[l], moe_b_dn[l])

    bp = x_prompt.shape[0]
    z_dn = jnp.zeros((bp, 2, DN_H, DN_DK, DN_DV), jnp.float32)
    z_rw = jnp.zeros((bp, 2, RW_H, RW_DH, RW_DH), jnp.float32)
    z_gla = jnp.zeros((bp, 2, GLA_H, GLA_DK, GLA_DV), jnp.float32)
    cond_ctx = c_ctx[None, :]
    xp = x_prompt
    st_dn, st_rw, st_gla = [], [], []
    for l in range(DEPTH):
        xp, s1, s2, s3 = run_layer(xp, cond_ctx, z_dn, z_rw, z_gla, l)
        st_dn.append(s1)
        st_rw.append(s2)
        st_gla.append(s3)
    y_prompt = rmsnorm(xp, norm_out)
    new_state_delta = jnp.stack(st_dn, axis=1)
    new_state_rwkv = jnp.stack(st_rw, axis=1)
    new_state_gla = jnp.stack(st_gla, axis=1)

    rows = x_sample.shape[1] // GRID_W
    xs = x_sample + grid_pos_embed(rows, D_MODEL).astype(x_sample.dtype)[None]
    for l in range(DEPTH):
        xs, _, _, _ = run_layer(xs, c, state_delta[:, l], state_rwkv[:, l], state_gla[:, l], l)
    y_sample = rmsnorm(xs, norm_out)
    return (y_prompt, y_sample, new_state_delta, new_state_rwkv, new_state_gla)
```

```python
import functools

import jax
import jax.numpy as jnp
import numpy as np
from jax import lax
from jax.experimental import pallas as pl
from jax.experimental.pallas import tpu as pltpu

F32 = jnp.float32
BF16 = jnp.bfloat16
HIGHEST = lax.Precision.HIGHEST

D_MODEL = 1024
DEPTH = 4
GRID_W = 64
POS_THETA = 10000.0
N_MOD = 6
RMS_EPS = 1e-6

DN_H, DN_DK, DN_DV = 4, 128, 128
DN_W = DN_H * DN_DV
CONV_K = 5
DN_CHUNK = 64

RW_H, RW_DH = 4, 64
RW_W = RW_H * RW_DH
RW_W_LORA, RW_A_LORA, RW_G_LORA = 64, 64, 128
RW_GN_EPS = 64e-5

GLA_H, GLA_DK, GLA_DV = 4, 32, 64
GLA_KW = GLA_H * GLA_DK
GLA_VW = GLA_H * GLA_DV
GLA_GK_LORA = 16
GLA_GATE_NORM = 16.0
GLA_CHUNK = 16

MIX_W = DN_W + RW_W + GLA_VW
DN_COLS = 4 * DN_W + 4 * DN_H
RW_COLS = 3 * RW_W + 2 * RW_W_LORA + 2 * RW_A_LORA + RW_G_LORA
GLA_COLS = 2 * GLA_KW + 2 * GLA_VW + 2 * GLA_GK_LORA
IN_COLS = DN_COLS + RW_COLS + GLA_COLS

N_EXPERTS = 32
TOP_K = 4
D_FF = 1024
SWIGLU_LIMIT = 7.0
SWIGLU_ALPHA = 1.702

LANES = 128
SUBLANES = 8
VMEM_LIMIT = 56 * 1024 * 1024

DN_PAD = 17 * LANES
RW_OFF = DN_PAD
GLA_OFF = RW_OFF + RW_COLS
U_COLS = GLA_OFF + 7 * LANES

ROW_TILE = 256
MOE_BM = 256
NEG_BIG = -1e30


def _cparams(sem):
    return pltpu.CompilerParams(dimension_semantics=sem, vmem_limit_bytes=VMEM_LIMIT)


def _silu(x):
    return x * jax.nn.sigmoid(x)


def _mod_kernel(c_ref, w_ref, b_ref, o_ref):
    c = c_ref[...]
    o_ref[0] = jnp.dot(_silu(c), w_ref[0], precision=HIGHEST, preferred_element_type=F32) + b_ref[0]


def modulation(cond, w_mod, b_mod):
    depth = w_mod.shape[0]
    n_out = w_mod.shape[2]
    tn = 1536
    return pl.pallas_call(
        _mod_kernel,
        grid=(depth, n_out // tn),
        in_specs=[
            pl.BlockSpec((SUBLANES, D_MODEL), lambda l, j: (0, 0)),
            pl.BlockSpec((1, D_MODEL, tn), lambda l, j: (l, 0, j)),
            pl.BlockSpec((1, 1, tn), lambda l, j: (l, 0, j)),
        ],
        out_specs=pl.BlockSpec((1, SUBLANES, tn), lambda l, j: (l, 0, j)),
        out_shape=jax.ShapeDtypeStruct((depth, SUBLANES, n_out), F32),
        compiler_params=_cparams(("arbitrary", "arbitrary")),
        name="modulation",
    )(cond, w_mod, b_mod.reshape(depth, 1, n_out))


def _tile_cond(i, n_ctx_tiles, tiles_per_latent):
    return jnp.where(i < n_ctx_tiles, 0, 1 + (i - n_ctx_tiles) // tiles_per_latent)


def _inproj_kernel(gate_row, x_ref, d_ref, mg_ref, mod_ref, g_ref, w_ref, xo_ref, u_ref):
    x = x_ref[...]
    if gate_row is None:
        x = x + d_ref[...]
    else:
        x = x + mg_ref[0, gate_row:gate_row + 1, :] * d_ref[...]
    xo_ref[...] = x
    y = x * lax.rsqrt(jnp.mean(x * x, axis=-1, keepdims=True) + RMS_EPS) * g_ref[...]
    h = y * (1.0 + mod_ref[0, 1:2, :]) + mod_ref[0, 0:1, :]
    u_ref[...] = jnp.dot(h.astype(BF16), w_ref[...], preferred_element_type=F32)


def inproj(x, delta, mod_gate, mod, g, w, n_ctx_tiles, tiles_per_latent, gate_row):
    n = x.shape[0]
    cond_of = functools.partial(_tile_cond, n_ctx_tiles=n_ctx_tiles, tiles_per_latent=tiles_per_latent)
    kern = functools.partial(_inproj_kernel, gate_row)
    return pl.pallas_call(
        kern,
        grid=(n // ROW_TILE,),
        in_specs=[
            pl.BlockSpec((ROW_TILE, D_MODEL), lambda i: (i, 0)),
            pl.BlockSpec((ROW_TILE, D_MODEL), lambda i: (i, 0)),
            pl.BlockSpec((1, SUBLANES, D_MODEL), lambda i: (cond_of(i), 0, 0)),
            pl.BlockSpec((1, SUBLANES, D_MODEL), lambda i: (cond_of(i), 0, 0)),
            pl.BlockSpec((1, D_MODEL), lambda i: (0, 0)),
            pl.BlockSpec((D_MODEL, U_COLS), lambda i: (0, 0)),
        ],
        out_specs=[
            pl.BlockSpec((ROW_TILE, D_MODEL), lambda i: (i, 0)),
            pl.BlockSpec((ROW_TILE, U_COLS), lambda i: (i, 0)),
        ],
        out_shape=[
            jax.ShapeDtypeStruct((n, D_MODEL), F32),
            jax.ShapeDtypeStruct((n, U_COLS), F32),
        ],
        compiler_params=_cparams(("parallel",)),
        name="inproj",
    )(x, delta, mod_gate, mod, g, w)


def _outproj_kernel(o_ref, x_ref, mod_ref, g_ref, w_ref, rw_ref, rb_ref, x2_ref, h2_ref, lg_ref):
    mix = jnp.dot(o_ref[...].astype(BF16), w_ref[...], preferred_element_type=F32)
    x2 = x_ref[...] + mod_ref[0, 2:3, :] * mix
    x2_ref[...] = x2
    y = x2 * lax.rsqrt(jnp.mean(x2 * x2, axis=-1, keepdims=True) + RMS_EPS) * g_ref[...]
    h2 = y * (1.0 + mod_ref[0, 4:5, :]) + mod_ref[0, 3:4, :]
    h2_ref[...] = h2.astype(BF16)
    lg_ref[...] = jnp.dot(h2, rw_ref[...], precision=HIGHEST, preferred_element_type=F32) + rb_ref[...]


def outproj(o, x, mod, g, w, router_w, router_b, n_ctx_tiles, tiles_per_latent):
    n = x.shape[0]
    cond_of = functools.partial(_tile_cond, n_ctx_tiles=n_ctx_tiles, tiles_per_latent=tiles_per_latent)
    return pl.pallas_call(
        _outproj_kernel,
        grid=(n // ROW_TILE,),
        in_specs=[
            pl.BlockSpec((ROW_TILE, MIX_W), lambda i: (i, 0)),
            pl.BlockSpec((ROW_TILE, D_MODEL), lambda i: (i, 0)),
            pl.BlockSpec((1, SUBLANES, D_MODEL), lambda i: (cond_of(i), 0, 0)),
            pl.BlockSpec((1, D_MODEL), lambda i: (0, 0)),
            pl.BlockSpec((MIX_W, D_MODEL), lambda i: (0, 0)),
            pl.BlockSpec((D_MODEL, LANES), lambda i: (0, 0)),
            pl.BlockSpec((1, LANES), lambda i: (0, 0)),
        ],
        out_specs=[
            pl.BlockSpec((ROW_TILE, D_MODEL), lambda i: (i, 0)),
            pl.BlockSpec((ROW_TILE, D_MODEL), lambda i: (i, 0)),
            pl.BlockSpec((ROW_TILE, LANES), lambda i: (i, 0)),
        ],
        out_shape=[
            jax.ShapeDtypeStruct((n, D_MODEL), F32),
            jax.ShapeDtypeStruct((n, D_MODEL), BF16),
            jax.ShapeDtypeStruct((n, LANES), F32),
        ],
        compiler_params=_cparams(("parallel",)),
        name="outproj",
    )(o, x, mod, g, w, router_w, router_b)


def _final_kernel(x_ref, d_ref, mod_ref, g_ref, y_ref):
    x = x_ref[...] + mod_ref[0, 5:6, :] * d_ref[...]
    y_ref[...] = x * lax.rsqrt(jnp.mean(x * x, axis=-1, keepdims=True) + RMS_EPS) * g_ref[...]


def final_norm(x, delta, mod, g, n_ctx_tiles, tiles_per_latent):
    n = x.shape[0]
    cond_of = functools.partial(_tile_cond, n_ctx_tiles=n_ctx_tiles, tiles_per_latent=tiles_per_latent)
    return pl.pallas_call(
        _final_kernel,
        grid=(n // ROW_TILE,),
        in_specs=[
            pl.BlockSpec((ROW_TILE, D_MODEL), lambda i: (i, 0)),
            pl.BlockSpec((ROW_TILE, D_MODEL), lambda i: (i, 0)),
            pl.BlockSpec((1, SUBLANES, D_MODEL), lambda i: (cond_of(i), 0, 0)),
            pl.BlockSpec((1, D_MODEL), lambda i: (0, 0)),
        ],
        out_specs=pl.BlockSpec((ROW_TILE, D_MODEL), lambda i: (i, 0)),
        out_shape=jax.ShapeDtypeStruct((n, D_MODEL), F32),
        compiler_params=_cparams(("parallel",)),
        name="final_norm",
    )(x, delta, mod, g)


def _tri_masks(c):
    i = lax.broadcasted_iota(jnp.int32, (c, c), 0)
    j = lax.broadcasted_iota(jnp.int32, (c, c), 1)
    return i, j


def _dn_kernel(n_chunks, has_s0, q_ref, k_ref, v_ref, bg_ref, grow_ref, *rest):
    if has_s0:
        s0_ref, o_ref, sfin_ref, s_scr = rest
        s_scr[...] = s0_ref[0]
    else:
        o_ref, sfin_ref, s_scr = rest
        s_scr[...] = jnp.zeros_like(s_scr)
    c = DN_CHUNK
    o_ref[...] = jnp.zeros_like(o_ref)
    ii, jj = _tri_masks(c)
    eye = (ii == jj).astype(F32)
    incl = (ii >= jj, ii <= jj)
    strict = (ii > jj, ii < jj)
    cum_m = (incl[0].astype(F32), incl[1].astype(F32))

    def body(n, carry):
        for d in range(2):
            ch = n if d == 0 else n_chunks - 1 - n
            r0 = pl.multiple_of(ch * c, c)
            rows = pl.ds(r0, c)
            bg = bg_ref[0, rows, :]
            g_cum = jnp.dot(cum_m[d], bg, precision=HIGHEST, preferred_element_type=F32)
            g_rows = lax.dot_general(grow_ref[0, ch], cum_m[d], (((1,), (1,)), ((), ())),
                                     precision=HIGHEST, preferred_element_type=F32)
            last = c - 1 if d == 0 else 0
            for h in range(DN_H):
                idx = d * DN_H + h
                cols = slice(h * DN_DK, (h + 1) * DN_DK)
                qc = q_ref[0, rows, cols]
                kc = k_ref[0, rows, cols]
                vc = v_ref[0, rows, cols]
                beta = bg[:, idx:idx + 1]
                g_col = g_cum[:, 2 * DN_H + idx:2 * DN_H + idx + 1]
                g_row = g_rows[idx:idx + 1, :]
                g_tot = g_col[last:last + 1, :]
                decay = jnp.exp(jnp.where(incl[d], g_col - g_row, NEG_BIG))
                kb = kc.astype(BF16)
                kk = lax.dot_general(kb, kb, (((1,), (1,)), ((), ())), preferred_element_type=F32)
                low = jnp.where(strict[d], beta * kk * decay, 0.0)
                inv = eye - low
                pw = low
                for _ in range(5):
                    pw = jnp.dot(pw, pw, precision=HIGHEST, preferred_element_type=F32)
                    inv = inv + jnp.dot(inv, pw, precision=HIGHEST, preferred_element_type=F32)
                eg = jnp.exp(g_col)
                rhs = jnp.concatenate([beta * vc, (beta * eg) * kc], axis=1)
                sol = jnp.dot(inv, rhs, precision=HIGHEST, preferred_element_type=F32)
                u_t = sol[:, :DN_DV]
                w_k = sol[:, DN_DV:]
                qk = lax.dot_general(qc.astype(BF16), kb, (((1,), (1,)), ((), ())),
                                     preferred_element_type=F32) * decay
                q_g = qc * eg
                k_g = kc * jnp.exp(g_tot - g_col)
                s = s_scr[idx]
                sb = s.astype(BF16)
                u = u_t - jnp.dot(w_k.astype(BF16), sb, preferred_element_type=F32)
                ub = u.astype(BF16)
                o = (jnp.dot(q_g.astype(BF16), sb, preferred_element_type=F32)
                     + jnp.dot(qk.astype(BF16), ub, preferred_element_type=F32))
                s_scr[idx] = jnp.exp(g_tot) * s + lax.dot_general(
                    k_g.astype(BF16), ub, (((0,), (0,)), ((), ())), preferred_element_type=F32)
                o_ref[0, rows, cols] += o
        return carry

    lax.fori_loop(0, n_chunks, body, 0)
    sfin_ref[0] = s_scr[...]


def deltanet_core(q, k, v, bg, grow, s0):
    b, t, _ = q.shape
    n_chunks = t // DN_CHUNK
    nh2 = 2 * DN_H
    seq = lambda width: pl.BlockSpec((1, t, width), lambda i: (i, 0, 0))
    st = pl.BlockSpec((1, nh2, DN_DK, DN_DV), lambda i: (i, 0, 0, 0))
    in_specs = [seq(DN_W), seq(DN_W), seq(DN_W), seq(LANES),
                pl.BlockSpec((1, n_chunks, nh2, DN_CHUNK), lambda i: (i, 0, 0, 0))]
    args = [q, k, v, bg, grow]
    if s0 is not None:
        in_specs.append(st)
        args.append(s0)
    return pl.pallas_call(
        functools.partial(_dn_kernel, n_chunks, s0 is not None),
        grid=(b,),
        in_specs=in_specs,
        out_specs=[seq(DN_W), st],
        out_shape=[jax.ShapeDtypeStruct((b, t, DN_W), F32),
                   jax.ShapeDtypeStruct((b, nh2, DN_DK, DN_DV), F32)],
        scratch_shapes=[pltpu.VMEM((nh2, DN_DK, DN_DV), F32)],
        compiler_params=_cparams(("parallel",)),
        name="deltanet",
    )(*args)


def _rwkv_kernel(tt, r_ref, w_ref, k_ref, a_ref, b_ref, v_ref, s0_ref, y_ref, sfin_ref, s_scr):
    @pl.when(pl.program_id(0) == 0)
    def _():
        s_scr[...] = s0_ref[...]

    nk = RW_DH
    n_acc = 4

    def step(t, carry):
        vv = v_ref[t]
        acc = [None] * n_acc
        for k in range(nk):
            term = s_scr[k] * a_ref[t, k:k + 1, :]
            acc[k % n_acc] = term if acc[k % n_acc] is None else acc[k % n_acc] + term
        sa = (acc[0] + acc[1]) + (acc[2] + acc[3])
        acc = [None] * n_acc
        for k in range(nk):
            s = s_scr[k] * w_ref[t, k:k + 1, :] + sa * b_ref[t, k:k + 1, :] + vv * k_ref[t, k:k + 1, :]
            s_scr[k] = s
            term = s * r_ref[t, k:k + 1, :]
            acc[k % n_acc] = term if acc[k % n_acc] is None else acc[k % n_acc] + term
        y_ref[t] = (acc[0] + acc[1]) + (acc[2] + acc[3])
        return carry

    lax.fori_loop(0, tt, step, 0)

    @pl.when(pl.program_id(0) == pl.num_programs(0) - 1)
    def _():
        sfin_ref[...] = s_scr[...]


def rwkv_core(r, w, k, a, b, v, s0, tt):
    t = r.shape[0]
    vs = v.shape[1]
    row = pl.BlockSpec((tt, RW_DH, LANES), lambda i: (i, 0, 0))
    val = pl.BlockSpec((tt, vs, LANES), lambda i: (i, 0, 0))
    st = pl.BlockSpec((RW_DH, vs, LANES), lambda i: (0, 0, 0))
    return pl.pallas_call(
        functools.partial(_rwkv_kernel, tt),
        grid=(t // tt,),
        in_specs=[row, row, row, row, row, val, st],
        out_specs=[val, st],
        out_shape=[jax.ShapeDtypeStruct((t, vs, LANES), F32),
                   jax.ShapeDtypeStruct((RW_DH, vs, LANES), F32)],
        scratch_shapes=[pltpu.VMEM((RW_DH, vs, LANES), F32)],
        compiler_params=_cparams(("arbitrary",)),
        name="rwkv7",
    )(r, w, k, a, b, v, s0)


def _gla_kernel(n_chunks, q_ref, k_ref, v_ref, gk_ref, s0_ref, o_ref, sfin_ref, s_scr):
    c = GLA_CHUNK
    s_scr[...] = s0_ref[0]
    o_ref[...] = jnp.zeros_like(o_ref)
    ii, jj = _tri_masks(c)
    cum_m = ((ii >= jj).astype(F32), (ii <= jj).astype(F32))
    row_i = lax.broadcasted_iota(jnp.int32, (c, 1), 0)
    hk = lax.broadcasted_iota(jnp.int32, (GLA_KW, GLA_VW), 0) // GLA_DK
    hv = lax.broadcasted_iota(jnp.int32, (GLA_KW, GLA_VW), 1) // GLA_DV
    head_sum = (hk == hv).astype(BF16)
    bd_mask = (lax.broadcasted_iota(jnp.int32, (GLA_VW, GLA_KW), 0) // GLA_DV
               == lax.broadcasted_iota(jnp.int32, (GLA_VW, GLA_KW), 1) // GLA_DK)

    def body(n, carry):
        for d in range(2):
            ch = n if d == 0 else n_chunks - 1 - n
            r0 = pl.multiple_of(ch * c, c)
            rows = pl.ds(r0, c)
            qc = q_ref[0, rows, :]
            kc = k_ref[0, rows, :]
            vc = v_ref[0, rows, :]
            g = gk_ref[0, rows, d * GLA_KW:(d + 1) * GLA_KW]
            bc = jnp.dot(cum_m[d], g, precision=HIGHEST, preferred_element_type=F32)
            last = c - 1 if d == 0 else 0
            bc_last = bc[last:last + 1, :]
            terms = []
            for j in range(c):
                keep = (row_i >= j) if d == 0 else (row_i <= j)
                dec = jnp.exp(jnp.where(keep, bc - bc[j:j + 1, :], NEG_BIG))
                terms.append(qc * kc[j:j + 1, :] * dec)
            t_all = jnp.concatenate(terms, axis=0).astype(BF16)
            att = jnp.dot(t_all, head_sum, preferred_element_type=F32)
            o = None
            for j in range(c):
                term = att[j * c:(j + 1) * c, :] * vc[j:j + 1, :]
                o = term if o is None else o + term
            q_g = qc * jnp.exp(bc)
            k_g = kc * jnp.exp(bc_last - bc)
            st = s_scr[d]
            o = o + lax.dot_general(q_g.astype(BF16), st.astype(BF16), (((1,), (1,)), ((), ())),
                                    preferred_element_type=F32)
            upd = lax.dot_general(vc.astype(BF16), k_g.astype(BF16), (((0,), (0,)), ((), ())),
                                  preferred_element_type=F32)
            s_scr[d] = st * jnp.exp(bc_last) + jnp.where(bd_mask, upd, 0.0)
            o_ref[0, rows, :] += o
        return carry

    lax.fori_loop(0, n_chunks, body, 0)
    sfin_ref[0] = s_scr[...]


def gla_core(q, k, v, gk, s0t):
    b, t, _ = q.shape
    n_chunks = t // GLA_CHUNK
    seq = lambda width: pl.BlockSpec((1, t, width), lambda i: (i, 0, 0))
    st = pl.BlockSpec((1, 2, GLA_VW, GLA_KW), lambda i: (i, 0, 0, 0))
    return pl.pallas_call(
        functools.partial(_gla_kernel, n_chunks),
        grid=(b,),
        in_specs=[seq(GLA_KW), seq(GLA_KW), seq(GLA_VW), seq(2 * GLA_KW), st],
        out_specs=[seq(GLA_VW), st],
        out_shape=[jax.ShapeDtypeStruct((b, t, GLA_VW), F32),
                   jax.ShapeDtypeStruct((b, 2, GLA_VW, GLA_KW), F32)],
        scratch_shapes=[pltpu.VMEM((2, GLA_VW, GLA_KW), F32)],
        compiler_params=_cparams(("parallel",)),
        name="gla",
    )(q, k, v, gk, s0t)


def _moe_kernel(be_ref, nb_ref, x_ref, wgu_ref, bgu_ref, wdn_ref, bdn_ref, y_ref, wgu_bf, wdn_bf):
    i = pl.program_id(0)
    used = i < nb_ref[0]
    e = be_ref[i]
    prev = be_ref[jnp.maximum(i - 1, 0)]
    fresh = jnp.logical_or(i == 0, e != prev)

    @pl.when(jnp.logical_and(used, fresh))
    def _():
        wgu_bf[...] = wgu_ref[0].astype(BF16)
        wdn_bf[...] = wdn_ref[0].astype(BF16)

    @pl.when(used)
    def _():
        gu = jnp.dot(x_ref[...], wgu_bf[...], preferred_element_type=F32) + bgu_ref[0]
        gt = jnp.minimum(gu[:, :D_FF], SWIGLU_LIMIT)
        up = jnp.clip(gu[:, D_FF:], -SWIGLU_LIMIT, SWIGLU_LIMIT)
        act = (up + 1.0) * (gt * jax.nn.sigmoid(gt * SWIGLU_ALPHA))
        y_ref[...] = jnp.dot(act.astype(BF16), wdn_bf[...], preferred_element_type=F32) + bdn_ref[0]

    @pl.when(jnp.logical_not(used))
    def _():
        y_ref[...] = jnp.zeros_like(y_ref)


def moe_experts(block_e, n_used, buf, w_gu, b_gu, w_dn, b_dn):
    n_rows = buf.shape[0]
    n_blocks = n_rows // MOE_BM
    grid_spec = pltpu.PrefetchScalarGridSpec(
        num_scalar_prefetch=2,
        grid=(n_blocks,),
        in_specs=[
            pl.BlockSpec((MOE_BM, D_MODEL), lambda i, be, nb: (i, 0)),
            pl.BlockSpec((1, D_MODEL, 2 * D_FF), lambda i, be, nb: (be[i], 0, 0)),
            pl.BlockSpec((1, 1, 2 * D_FF), lambda i, be, nb: (be[i], 0, 0)),
            pl.BlockSpec((1, D_FF, D_MODEL), lambda i, be, nb: (be[i], 0, 0)),
            pl.BlockSpec((1, 1, D_MODEL), lambda i, be, nb: (be[i], 0, 0)),
        ],
        out_specs=pl.BlockSpec((MOE_BM, D_MODEL), lambda i, be, nb: (i, 0)),
        scratch_shapes=[pltpu.VMEM((D_MODEL, 2 * D_FF), BF16), pltpu.VMEM((D_FF, D_MODEL), BF16)],
    )
    return pl.pallas_call(
        _moe_kernel,
        grid_spec=grid_spec,
        out_shape=jax.ShapeDtypeStruct((n_rows, D_MODEL), F32),
        compiler_params=_cparams(("arbitrary",)),
        name="moe_experts",
    )(block_e, n_used, buf, w_gu, b_gu.reshape(N_EXPERTS, 1, 2 * D_FF), w_dn, b_dn.reshape(N_EXPERTS, 1, D_MODEL))


def moe_ffn(h2, logits, w_gu, b_gu, w_dn, b_dn):
    n_tok = h2.shape[0]
    n_assign = n_tok * TOP_K
    n_blocks = n_assign // MOE_BM + N_EXPERTS
    top_val, top_idx = lax.top_k(logits[:, :N_EXPERTS], TOP_K)
    gates = jax.nn.softmax(top_val, axis=-1)
    e_flat = top_idx.reshape(-1).astype(jnp.int32)
    ar = jnp.arange(n_assign, dtype=jnp.int32)
    e_sorted, order = lax.sort((e_flat, ar), num_keys=1, is_stable=True)
    counts = jnp.sum(e_flat[:, None] == jnp.arange(N_EXPERTS, dtype=jnp.int32)[None, :], axis=0, dtype=jnp.int32)
    padded = (counts + MOE_BM - 1) // MOE_BM * MOE_BM
    pad_end = jnp.cumsum(padded)
    pad_start = pad_end - padded
    start = jnp.cumsum(counts) - counts
    dest_sorted = pad_start[e_sorted] + ar - start[e_sorted]
    _, pos = lax.sort((order, dest_sorted), num_keys=1)
    blk_start = jnp.arange(n_blocks, dtype=jnp.int32) * MOE_BM
    n_used = (pad_end[-1] // MOE_BM).astype(jnp.int32)
    block_e = jnp.minimum(jnp.searchsorted(pad_end, blk_start, side='right'), N_EXPERTS - 1).astype(jnp.int32)
    block_e = jnp.where(jnp.arange(n_blocks) < n_used, block_e, block_e[jnp.maximum(n_used - 1, 0)])
    row = jnp.arange(n_blocks * MOE_BM, dtype=jnp.int32)
    row_e = block_e[row // MOE_BM]
    local = row - pad_start[row_e]
    valid = local < counts[row_e]
    src = order[jnp.clip(start[row_e] + local, 0, n_assign - 1)] // TOP_K
    buf = jnp.where(valid[:, None], h2[src], jnp.zeros((), h2.dtype))
    y_buf = moe_experts(block_e, n_used.reshape(1), buf, w_gu, b_gu, w_dn, b_dn)
    y_sel = y_buf[pos].reshape(n_tok, TOP_K, D_MODEL)
    return jnp.sum(y_sel * gates[:, :, None], axis=1)


def _l2norm(x, eps=1e-6):
    return x * lax.rsqrt(jnp.sum(x * x, axis=-1, keepdims=True) + eps)


def _rmsnorm(x, g, eps=RMS_EPS):
    return x * lax.rsqrt(jnp.mean(x * x, axis=-1, keepdims=True) + eps) * g


def _centred_dwconv(u, w):
    k = w.shape[0]
    half = k // 2
    t = u.shape[1]
    up = jnp.pad(u, ((0, 0), (half, half), (0, 0)))
    out = up[:, 0:t] * w[0]
    for j in range(1, k):
        out = out + up[:, j:j + t] * w[j]
    return out


def _centred_shift(u):
    z = jnp.zeros_like(u[:, :1])
    return 0.5 * (jnp.concatenate([z, u[:, :-1]], axis=1) + jnp.concatenate([u[:, 1:], z], axis=1))


def deltanet_mixer(dn_u, conv_w, a_log, dt_bias, norm_w, s0):
    b, t, _ = dn_u.shape
    qkv = _silu(_centred_dwconv(dn_u[..., :3 * DN_W], conv_w))
    z = dn_u[..., 3 * DN_W:4 * DN_W]
    bt = dn_u[..., 4 * DN_W:4 * DN_W + 2 * DN_H]
    at = dn_u[..., 4 * DN_W + 2 * DN_H:4 * DN_W + 4 * DN_H]
    q = (_l2norm(qkv[..., :DN_W].reshape(b, t, DN_H, DN_DK)) * DN_DK ** -0.5).reshape(b, t, DN_W)
    k = _l2norm(qkv[..., DN_W:2 * DN_W].reshape(b, t, DN_H, DN_DK)).reshape(b, t, DN_W)
    v = qkv[..., 2 * DN_W:]
    beta = jax.nn.sigmoid(bt)
    g = (-jnp.exp(a_log) * jax.nn.softplus(at.reshape(b, t, 2, DN_H) + dt_bias)).reshape(b, t, 2 * DN_H)
    grow = g.reshape(b, t // DN_CHUNK, DN_CHUNK, 2 * DN_H).transpose(0, 1, 3, 2)
    s0k = None if s0 is None else s0.reshape(b, 2 * DN_H, DN_DK, DN_DV)
    bg = jnp.concatenate([beta, g, jnp.zeros((b, t, LANES - 4 * DN_H), F32)], axis=-1)
    o, sfin = deltanet_core(q, k, v, bg, grow, s0k)
    o = _rmsnorm(o.reshape(b, t, DN_H, DN_DV), norm_w) * _silu(z.reshape(b, t, DN_H, DN_DV))
    return o.reshape(b, t, DN_W), sfin.reshape(b, 2, DN_H, DN_DK, DN_DV)


def rwkv7_mixer(rw_u, mu, w0, w2, a0, a2, g2, k_k, k_a, r_k, ln_w, ln_b, s0):
    b, t, _ = rw_u.shape
    rw_u = rw_u + mu * (_centred_shift(rw_u) - rw_u)
    r = rw_u[..., :RW_W]
    k = rw_u[..., RW_W:2 * RW_W]
    v = rw_u[..., 2 * RW_W:3 * RW_W]
    o1 = 3 * RW_W
    wd = rw_u[..., o1:o1 + 2 * RW_W_LORA].reshape(b, t, 2, RW_W_LORA)
    ad = rw_u[..., o1 + 2 * RW_W_LORA:o1 + 2 * RW_W_LORA + 2 * RW_A_LORA].reshape(b, t, 2, RW_A_LORA)
    gd = rw_u[..., o1 + 2 * RW_W_LORA + 2 * RW_A_LORA:]
    w_log = -jax.nn.softplus(-(w0 + jnp.einsum('btdr,drc->btdc', jnp.tanh(wd), w2))) - 0.5
    decay = jnp.exp(-jnp.exp(w_log))
    a = jax.nn.sigmoid(a0 + jnp.einsum('btdr,drc->btdc', ad, a2))
    gate = jax.nn.sigmoid(gd) @ g2
    heads = lambda x: x.reshape(*x.shape[:-1], RW_H, RW_DH)
    kk = _l2norm(heads(k * k_k))
    k_dir = k[:, :, None, :] * (1.0 + (a - 1.0) * k_a)
    r_h, k_h, v_h = heads(r), heads(k), heads(v)

    n_chain = 2 * b * RW_H
    vg = LANES // n_chain
    vs = RW_DH // vg

    def both(x_f, x_b):
        return jnp.stack([x_f, jnp.flip(x_b, axis=1)], axis=0)

    def rows(x2):
        y = x2.transpose(2, 4, 0, 1, 3).reshape(t, RW_DH, n_chain)
        return jnp.tile(y, (1, 1, vg))

    r_l = rows(both(r_h, r_h))
    w_l = rows(both(heads(decay[:, :, 0]), heads(decay[:, :, 1])))
    k_l = rows(both(heads(k_dir[:, :, 0]), heads(k_dir[:, :, 1])))
    a_l = rows(both(-kk, -kk))
    b_l = rows(both(kk * heads(a[:, :, 0]), kk * heads(a[:, :, 1])))
    v2 = both(v_h, v_h).reshape(2, b, t, RW_H, vg, vs)
    v_l = v2.transpose(2, 5, 4, 0, 1, 3).reshape(t, vs, LANES)
    s_l = s0.reshape(b, 2, RW_H, vg, vs, RW_DH).transpose(5, 4, 3, 1, 0, 2).reshape(RW_DH, vs, LANES)
    y_l, sfin_l = rwkv_core(r_l, w_l, k_l, a_l, b_l, v_l, s_l, tt=32)
    y2 = y_l.reshape(t, vs, vg, 2, b, RW_H).transpose(3, 4, 0, 5, 2, 1).reshape(2, b, t, RW_H, RW_DH)
    y = y2[0] + jnp.flip(y2[1], axis=1)
    sfin = sfin_l.reshape(RW_DH, vs, vg, 2, b, RW_H).transpose(4, 3, 5, 2, 1, 0).reshape(b, 2, RW_H, RW_DH, RW_DH)
    m = jnp.mean(y, axis=-1, keepdims=True)
    var = jnp.mean(jnp.square(y - m), axis=-1, keepdims=True)
    yn = ((y - m) * lax.rsqrt(var + RW_GN_EPS)).reshape(b, t, RW_W) * ln_w + ln_b
    bonus = (jnp.sum(r_h * k_h * r_k, axis=-1, keepdims=True) * v_h).reshape(b, t, RW_W)
    return (yn + bonus) * gate, sfin


def gla_mixer(gla_u, gk2, gk_b, norm_w, s0):
    b, t, _ = gla_u.shape
    q = gla_u[..., :GLA_KW] * GLA_DK ** -0.5
    k = gla_u[..., GLA_KW:2 * GLA_KW]
    v = gla_u[..., 2 * GLA_KW:2 * GLA_KW + GLA_VW]
    g = gla_u[..., 2 * GLA_KW + GLA_VW:2 * GLA_KW + 2 * GLA_VW]
    gkd = gla_u[..., 2 * GLA_KW + 2 * GLA_VW:2 * GLA_KW + 2 * GLA_VW + 2 * GLA_GK_LORA]
    gkd = gkd.reshape(b, t, 2, GLA_GK_LORA)
    gk = jax.nn.log_sigmoid(jnp.einsum('btdr,drc->btdc', gkd, gk2) + gk_b) / GLA_GATE_NORM
    gk = gk.reshape(b, t, 2 * GLA_KW)
    eye_h = jnp.eye(GLA_H, dtype=F32)
    s0t = jnp.einsum('bdhkv,hg->bdhvgk', s0, eye_h).reshape(b, 2, GLA_VW, GLA_KW)
    o, sfin_t = gla_core(q, k, v, gk, s0t)
    sf = sfin_t.reshape(b, 2, GLA_H, GLA_DV, GLA_H, GLA_DK)
    sfin = jnp.einsum('bdhvhk->bdhkv', sf)
    o = _rmsnorm(o.reshape(b, t, GLA_H, GLA_DV), norm_w) * _silu(g.reshape(b, t, GLA_H, GLA_DV))
    return o.reshape(b, t, GLA_VW), sfin


def grid_pos_embed(rows, dim):
    t = jnp.arange(rows * GRID_W)
    r = (t // GRID_W).astype(F32)
    col = (t % GRID_W).astype(F32)
    quarter = dim // 4
    omega = 1.0 / (POS_THETA ** (jnp.arange(quarter, dtype=F32) / quarter))
    er = r[:, None] * omega
    ec = col[:, None] * omega
    return jnp.concatenate([jnp.sin(er), jnp.cos(er), jnp.sin(ec), jnp.cos(ec)], axis=-1)


def _pad_cols(w_in):
    z = lambda n: jnp.zeros(w_in.shape[:-1] + (n,), w_in.dtype)
    return jnp.concatenate([
        w_in[..., :DN_COLS], z(DN_PAD - DN_COLS),
        w_in[..., DN_COLS:DN_COLS + RW_COLS],
        w_in[..., DN_COLS + RW_COLS:], z(U_COLS - GLA_OFF - GLA_COLS)], axis=-1)


def kernel(x_prompt, x_sample, state_delta, state_rwkv, state_gla, c, c_ctx, w_mod, b_mod, norm_mix, norm_ffn, norm_out, w_in, w_out, dn_conv, dn_a_log, dn_dt_bias, dn_norm, rw_mu, rw_w0, rw_w2, rw_a0, rw_a2, rw_g2, rw_k_k, rw_k_a, rw_r_k, rw_ln_w, rw_ln_b, gla_gk2, gla_gk_b, gla_norm, router_w, router_b, moe_w_gu, moe_b_gu, moe_w_dn, moe_b_dn):
    bp, tp, d = x_prompt.shape
    bs, ts, _ = x_sample.shape
    depth = w_in.shape[0]
    n_ctx = bp * tp
    n_lat = bs * ts
    n_ctx_tiles = n_ctx // ROW_TILE
    tiles_per_latent = ts // ROW_TILE
    tiles = dict(n_ctx_tiles=n_ctx_tiles, tiles_per_latent=tiles_per_latent)

    cond = jnp.concatenate([c_ctx[None, :], c, jnp.zeros((SUBLANES - 1 - bs, d), F32)], axis=0)
    mod_all = modulation(cond, w_mod, b_mod).reshape(depth, SUBLANES, N_MOD, d)
    mod_all = jnp.pad(mod_all, ((0, 0), (0, 0), (0, SUBLANES - N_MOD), (0, 0)))

    w_in_p = _pad_cols(w_in).astype(BF16)
    w_out_b = w_out.astype(BF16)
    router_w_p = jnp.pad(router_w, ((0, 0), (0, 0), (0, LANES - N_EXPERTS)))
    router_b_p = jnp.pad(router_b, ((0, 0), (0, LANES - N_EXPERTS)))[:, None, :]

    pos = grid_pos_embed(ts // GRID_W, d)
    x = jnp.concatenate([x_prompt.reshape(n_ctx, d), x_sample.reshape(n_lat, d)], axis=0)
    delta = jnp.concatenate([jnp.zeros((n_ctx, d), F32), jnp.tile(pos, (bs, 1))], axis=0)

    z_rw = jnp.zeros((bp, 2, RW_H, RW_DH, RW_DH), F32)
    z_gla = jnp.zeros((bp, 2, GLA_H, GLA_DK, GLA_DV), F32)
    st_dn, st_rw, st_gla = [], [], []
    for l in range(depth):
        mod = mod_all[l]
        x, u = inproj(x, delta, mod_all[max(l - 1, 0)], mod, norm_mix[l][None, :], w_in_p[l],
                      gate_row=None if l == 0 else 5, **tiles)
        outs = []
        for (lo, b, t, s_dn, s_rw, s_gla) in (
                (0, bp, tp, None, z_rw, z_gla),
                (n_ctx, bs, ts, state_delta[:, l], state_rwkv[:, l], state_gla[:, l])):
            us = u[lo:lo + b * t].reshape(b, t, U_COLS)
            o_dn, f_dn = deltanet_mixer(us[..., :DN_PAD], dn_conv[l], dn_a_log[l], dn_dt_bias[l], dn_norm[l], s_dn)
            o_rw, f_rw = rwkv7_mixer(us[..., RW_OFF:GLA_OFF], rw_mu[l], rw_w0[l], rw_w2[l], rw_a0[l], rw_a2[l],
                                     rw_g2[l], rw_k_k[l], rw_k_a[l], rw_r_k[l], rw_ln_w[l], rw_ln_b[l], s_rw)
            o_gla, f_gla = gla_mixer(us[..., GLA_OFF:], gla_gk2[l], gla_gk_b[l], gla_norm[l], s_gla)
            outs.append(jnp.concatenate([o_dn, o_rw, o_gla], axis=-1).reshape(b * t, MIX_W))
            if lo == 0:
                st_dn.append(f_dn)
                st_rw.append(f_rw)
                st_gla.append(f_gla)
        o = jnp.concatenate(outs, axis=0)
        x, h2, logits = outproj(o, x, mod, norm_ffn[l][None, :], w_out_b[l], router_w_p[l], router_b_p[l], **tiles)
        delta = moe_ffn(h2, logits, moe_w_gu[l], moe_b_gu[l], moe_w_dn[l], moe_b_dn[l])
    y = final_norm(x, delta, mod_all[depth - 1], norm_out[None, :], **tiles)
    y_prompt = y[:n_ctx].reshape(bp, tp, d)
    y_sample = y[n_ctx:].reshape(bs, ts, d)
    return (y_prompt, y_sample, jnp.stack(st_dn, axis=1), jnp.stack(st_rw, axis=1), jnp.stack(st_gla, axis=1))
```

```python
import functools

import jax
import jax.numpy as jnp
from jax import lax
from jax.experimental import pallas as pl
from jax.experimental.pallas import tpu as pltpu

F32 = jnp.float32
BF16 = jnp.bfloat16
HIGHEST = lax.Precision.HIGHEST

D_MODEL = 1024
GRID_W = 64
POS_THETA = 10000.0
N_MOD = 6
RMS_EPS = 1e-6

DN_H, DN_DK, DN_DV = 4, 128, 128
DN_W = DN_H * DN_DV
CONV_K = 5

RW_H, RW_DH = 4, 64
RW_W = RW_H * RW_DH
RW_W_LORA, RW_A_LORA, RW_G_LORA = 64, 64, 128
RW_GN_EPS = 64e-5

GLA_H, GLA_DK, GLA_DV = 4, 32, 64
GLA_KW = GLA_H * GLA_DK
GLA_VW = GLA_H * GLA_DV
GLA_GK_LORA = 16
GLA_GATE_NORM = 16.0
GLA_CHUNK = 16

MIX_W = DN_W + RW_W + GLA_VW
DN_COLS = 4 * DN_W + 4 * DN_H
RW_COLS = 3 * RW_W + 2 * RW_W_LORA + 2 * RW_A_LORA + RW_G_LORA
GLA_COLS = 2 * GLA_KW + 2 * GLA_VW + 2 * GLA_GK_LORA

N_EXPERTS = 32
TOP_K = 4
D_FF = 1024
SWIGLU_LIMIT = 7.0
SWIGLU_ALPHA = 1.702

LANES = 128
SUBLANES = 8
VMEM_LIMIT = 56 * 1024 * 1024

DN_PAD = 17 * LANES
RW_PAD = RW_COLS
GLA_PAD = 7 * LANES
U_COLS = DN_PAD + RW_PAD + GLA_PAD

ROW_TILE = 256
MOE_BM = 256
CHUNK = 64
PACK = DN_H * CHUNK
N_MERGE = CHUNK.bit_length() - 1
N_MASKS = 4 + N_MERGE
HALO = SUBLANES
NEG_BIG = -1e30


def _cparams(sem):
    return pltpu.CompilerParams(dimension_semantics=sem, vmem_limit_bytes=VMEM_LIMIT)


def _silu(x):
    return x * jax.nn.sigmoid(x)


def _softplus(x):
    return jnp.maximum(x, 0.0) + jnp.log(1.0 + jnp.exp(-jnp.abs(x)))


def _bdot(a, b):
    return jnp.dot(a.astype(BF16), b.astype(BF16), preferred_element_type=F32)


def _bdot_nt(a, b):
    return lax.dot_general(a.astype(BF16), b.astype(BF16), (((1,), (1,)), ((), ())),
                           preferred_element_type=F32)


def _bdot_tn(a, b):
    return lax.dot_general(a.astype(BF16), b.astype(BF16), (((0,), (0,)), ((), ())),
                           preferred_element_type=F32)


def _hdot(a, b):
    return jnp.dot(a, b, precision=HIGHEST, preferred_element_type=F32)


def _mod_kernel(c_ref, w_ref, b_ref, o_ref):
    o_ref[0] = _hdot(_silu(c_ref[...]), w_ref[0]) + b_ref[0]


def modulation(cond, w_mod, b_mod):
    depth = w_mod.shape[0]
    n_out = w_mod.shape[2]
    tn = 1536
    return pl.pallas_call(
        _mod_kernel,
        grid=(depth, n_out // tn),
        in_specs=[
            pl.BlockSpec((SUBLANES, D_MODEL), lambda l, j: (0, 0)),
            pl.BlockSpec((1, D_MODEL, tn), lambda l, j: (l, 0, j)),
            pl.BlockSpec((1, 1, tn), lambda l, j: (l, 0, j)),
        ],
        out_specs=pl.BlockSpec((1, SUBLANES, tn), lambda l, j: (l, 0, j)),
        out_shape=jax.ShapeDtypeStruct((depth, SUBLANES, n_out), F32),
        compiler_params=_cparams(("arbitrary", "arbitrary")),
        name="modulation",
    )(cond, w_mod, b_mod.reshape(depth, 1, n_out))


def _tile_cond(i, n_ctx_tiles, tiles_per_latent):
    return jnp.where(i < n_ctx_tiles, 0, 1 + (i - n_ctx_tiles) // tiles_per_latent)


def _inproj_kernel(gate_row, x_ref, d_ref, mg_ref, mod_ref, g_ref, w_ref, xo_ref, udn_ref, urw_ref, ugla_ref):
    x = x_ref[...]
    if gate_row is None:
        x = x + d_ref[...]
    else:
        x = x + mg_ref[0, gate_row:gate_row + 1, :] * d_ref[...]
    xo_ref[...] = x
    y = x * lax.rsqrt(jnp.mean(x * x, axis=-1, keepdims=True) + RMS_EPS) * g_ref[...]
    h = (y * (1.0 + mod_ref[0, 1:2, :]) + mod_ref[0, 0:1, :]).astype(BF16)
    udn_ref[...] = jnp.dot(h, w_ref[:, :DN_PAD], preferred_element_type=F32)
    urw_ref[...] = jnp.dot(h, w_ref[:, DN_PAD:DN_PAD + RW_PAD], preferred_element_type=F32)
    ugla_ref[...] = jnp.dot(h, w_ref[:, DN_PAD + RW_PAD:], preferred_element_type=F32)


def inproj(x, delta, mod_gate, mod, g, w, n_ctx_tiles, tiles_per_latent, gate_row):
    n = x.shape[0]
    cond_of = functools.partial(_tile_cond, n_ctx_tiles=n_ctx_tiles, tiles_per_latent=tiles_per_latent)
    rows = lambda width: pl.BlockSpec((ROW_TILE, width), lambda i: (i, 0))
    modspec = pl.BlockSpec((1, SUBLANES, D_MODEL), lambda i: (cond_of(i), 0, 0))
    return pl.pallas_call(
        functools.partial(_inproj_kernel, gate_row),
        grid=(n // ROW_TILE,),
        in_specs=[rows(D_MODEL), rows(D_MODEL), modspec, modspec,
                  pl.BlockSpec((1, D_MODEL), lambda i: (0, 0)),
                  pl.BlockSpec((D_MODEL, U_COLS), lambda i: (0, 0))],
        out_specs=[rows(D_MODEL), rows(DN_PAD), rows(RW_PAD), rows(GLA_PAD)],
        out_shape=[jax.ShapeDtypeStruct((n, D_MODEL), F32),
                   jax.ShapeDtypeStruct((n, DN_PAD), F32),
                   jax.ShapeDtypeStruct((n, RW_PAD), F32),
                   jax.ShapeDtypeStruct((n, GLA_PAD), F32)],
        compiler_params=_cparams(("parallel",)),
        name="inproj",
    )(x, delta, mod_gate, mod, g, w)


def _outproj_kernel(odn_ref, orw_ref, ogla_ref, x_ref, mod_ref, g_ref, w_ref, rw_ref, rb_ref,
                    x2_ref, h2_ref, lg_ref):
    mix = (jnp.dot(odn_ref[...].astype(BF16), w_ref[:DN_W, :], preferred_element_type=F32)
           + jnp.dot(orw_ref[...].astype(BF16), w_ref[DN_W:DN_W + RW_W, :], preferred_element_type=F32)
           + jnp.dot(ogla_ref[...].astype(BF16), w_ref[DN_W + RW_W:, :], preferred_element_type=F32))
    x2 = x_ref[...] + mod_ref[0, 2:3, :] * mix
    x2_ref[...] = x2
    y = x2 * lax.rsqrt(jnp.mean(x2 * x2, axis=-1, keepdims=True) + RMS_EPS) * g_ref[...]
    h2 = y * (1.0 + mod_ref[0, 4:5, :]) + mod_ref[0, 3:4, :]
    h2_ref[...] = h2.astype(BF16)
    lg_ref[...] = _hdot(h2, rw_ref[...]) + rb_ref[...]


def outproj(o_dn, o_rw, o_gla, x, mod, g, w, router_w, router_b, n_ctx_tiles, tiles_per_latent):
    n = x.shape[0]
    cond_of = functools.partial(_tile_cond, n_ctx_tiles=n_ctx_tiles, tiles_per_latent=tiles_per_latent)
    rows = lambda width: pl.BlockSpec((ROW_TILE, width), lambda i: (i, 0))
    return pl.pallas_call(
        _outproj_kernel,
        grid=(n // ROW_TILE,),
        in_specs=[rows(DN_W), rows(RW_W), rows(GLA_VW), rows(D_MODEL),
                  pl.BlockSpec((1, SUBLANES, D_MODEL), lambda i: (cond_of(i), 0, 0)),
                  pl.BlockSpec((1, D_MODEL), lambda i: (0, 0)),
                  pl.BlockSpec((MIX_W, D_MODEL), lambda i: (0, 0)),
                  pl.BlockSpec((D_MODEL, LANES), lambda i: (0, 0)),
                  pl.BlockSpec((1, LANES), lambda i: (0, 0))],
        out_specs=[rows(D_MODEL), rows(D_MODEL), rows(LANES)],
        out_shape=[jax.ShapeDtypeStruct((n, D_MODEL), F32),
                   jax.ShapeDtypeStruct((n, D_MODEL), BF16),
                   jax.ShapeDtypeStruct((n, LANES), F32)],
        compiler_params=_cparams(("parallel",)),
        name="outproj",
    )(o_dn, o_rw, o_gla, x, mod, g, w, router_w, router_b)


def _final_kernel(x_ref, d_ref, mod_ref, g_ref, y_ref):
    x = x_ref[...] + mod_ref[0, 5:6, :] * d_ref[...]
    y_ref[...] = x * lax.rsqrt(jnp.mean(x * x, axis=-1, keepdims=True) + RMS_EPS) * g_ref[...]


def final_norm(x, delta, mod, g, n_ctx_tiles, tiles_per_latent):
    n = x.shape[0]
    cond_of = functools.partial(_tile_cond, n_ctx_tiles=n_ctx_tiles, tiles_per_latent=tiles_per_latent)
    rows = pl.BlockSpec((ROW_TILE, D_MODEL), lambda i: (i, 0))
    return pl.pallas_call(
        _final_kernel,
        grid=(n // ROW_TILE,),
        in_specs=[rows, rows,
                  pl.BlockSpec((1, SUBLANES, D_MODEL), lambda i: (cond_of(i), 0, 0)),
                  pl.BlockSpec((1, D_MODEL), lambda i: (0, 0))],
        out_specs=rows,
        out_shape=jax.ShapeDtypeStruct((n, D_MODEL), F32),
        compiler_params=_cparams(("parallel",)),
        name="final_norm",
    )(x, delta, mod, g)


def _chunk_masks(mask_scr):
    r = lax.broadcasted_iota(jnp.int32, (PACK, PACK), 0)
    c = lax.broadcasted_iota(jnp.int32, (PACK, PACK), 1)
    same = (r // CHUNK) == (c // CHUNK)
    tr = r % CHUNK
    tc = c % CHUNK
    mask_scr[0] = jnp.where(same & (tc <= tr), 0.0, NEG_BIG)
    mask_scr[1] = jnp.where(same & (tc >= tr), 0.0, NEG_BIG)
    mask_scr[2] = (same & (tc < tr)).astype(F32)
    mask_scr[3] = (same & (tc > tr)).astype(F32)
    for i in range(N_MERGE):
        m = 1 << i
        mask_scr[4 + i] = (((r // (2 * m)) == (c // (2 * m))) & ((r // m) != (c // m))).astype(F32)


def _cum_matrix(d):
    i = lax.broadcasted_iota(jnp.int32, (CHUNK, CHUNK), 0)
    j = lax.broadcasted_iota(jnp.int32, (CHUNK, CHUNK), 1)
    return ((j <= i) if d == 0 else (j >= i)).astype(F32)


def _tri_inverse(a, mask_scr):
    n = a * mask_scr[4]
    for i in range(1, N_MERGE):
        c = a * mask_scr[4 + i]
        x = c + _bdot(c, n)
        n = n + x + _bdot(n, x)
    return n


def _fill_halo(pad_scr, src_ref, t, width):
    zeros = jnp.zeros((HALO, width), F32)
    pad_scr[0:HALO, :] = zeros
    pad_scr[HALO + t:2 * HALO + t, :] = zeros

    def copy(i, carry):
        r0 = pl.multiple_of(i * CHUNK, CHUNK)
        pad_scr[pl.ds(HALO + r0, CHUNK), :] = src_ref[pl.ds(r0, CHUNK), :width]
        return carry

    lax.fori_loop(0, t // CHUNK, copy, 0)


def _dn_kernel(t, has_s0, u_ref, cw_ref, gp_ref, nw_ref, *rest):
    if has_s0:
        s0_ref, o_ref, sfin_ref = rest[:3]
        scr = rest[3:]
    else:
        o_ref, sfin_ref = rest[:2]
        scr = rest[2:]
    pad_scr, q_scr, k_scr, v_scr, bg_scr, acc_scr, s_scr, mask_scr = scr
    n_chunks = t // CHUNK
    qkv_w = 3 * DN_W

    if has_s0:
        s_scr[...] = s0_ref[0]
    else:
        s_scr[...] = jnp.zeros_like(s_scr)
    acc_scr[...] = jnp.zeros_like(acc_scr)
    _chunk_masks(mask_scr)
    _fill_halo(pad_scr, u_ref, t, qkv_w)

    lane = lax.broadcasted_iota(jnp.int32, (1, LANES), 1)
    neg_a = -jnp.exp(gp_ref[0:1, :])
    dt_b = gp_ref[1:2, :]

    def prep(i, carry):
        r0 = pl.multiple_of(i * CHUNK, CHUNK)
        rows = pl.ds(r0, CHUNK)
        for cg in range(qkv_w // LANES):
            cols = slice(cg * LANES, (cg + 1) * LANES)
            win = pad_scr[pl.ds(r0, CHUNK + 2 * HALO), cols]
            acc = None
            for j in range(CONV_K):
                off = HALO - CONV_K // 2 + j
                term = cw_ref[j:j + 1, cols] * win[off:off + CHUNK, :]
                acc = term if acc is None else acc + term
            y = _silu(acc)
            which, hc = divmod(cg, DN_H)
            hcols = slice(hc * LANES, (hc + 1) * LANES)
            if which < 2:
                y = y * lax.rsqrt(jnp.sum(y * y, axis=-1, keepdims=True) + 1e-6)
            if which == 0:
                q_scr[rows, hcols] = y * DN_DK ** -0.5
            elif which == 1:
                k_scr[rows, hcols] = y
            else:
                v_scr[rows, hcols] = y
        ub = u_ref[rows, 4 * DN_W:4 * DN_W + LANES]
        bg_scr[rows, :] = jnp.where(lane < 2 * DN_H, jax.nn.sigmoid(ub), neg_a * _softplus(ub + dt_b))
        return carry

    lax.fori_loop(0, n_chunks, prep, 0)

    def stack(ref, rows):
        return jnp.concatenate([ref[rows, h * LANES:(h + 1) * LANES] for h in range(DN_H)], axis=0)

    def body(n, carry):
        for d in range(2):
            ch = n if d == 0 else n_chunks - 1 - n
            rows = pl.ds(pl.multiple_of(ch * CHUNK, CHUNK), CHUNK)
            last = CHUNK - 1 if d == 0 else 0
            bg = bg_scr[rows, :]
            g_cum = _hdot(_cum_matrix(d), bg)
            lanes_g = [2 * DN_H + d * DN_H + h for h in range(DN_H)]
            beta = jnp.concatenate([bg[:, d * DN_H + h:d * DN_H + h + 1] for h in range(DN_H)], axis=0)
            g_col = jnp.concatenate([g_cum[:, l:l + 1] for l in lanes_g], axis=0)
            g_tot = jnp.concatenate(
                [jnp.broadcast_to(g_cum[last:last + 1, l:l + 1], (CHUNK, 1)) for l in lanes_g], axis=0)
            g_row = jnp.transpose(jnp.broadcast_to(g_col, (PACK, LANES)))[0:1, :]
            decay = jnp.exp(g_col - g_row + mask_scr[d])
            qc = stack(q_scr, rows)
            kc = stack(k_scr, rows)
            vc = stack(v_scr, rows)
            qkk = _bdot_nt(jnp.concatenate([qc, kc], axis=0), kc)
            qk = qkk[:PACK] * decay
            low = (beta * qkk[PACK:]) * (decay * mask_scr[2 + d])
            nm = _tri_inverse(-low, mask_scr)
            eg = jnp.exp(g_col)
            rhs = jnp.concatenate([beta * vc, (beta * eg) * kc], axis=1)
            sol = rhs + _bdot(nm, rhs)
            u_t = sol[:, :DN_DV]
            w_k = sol[:, DN_DV:]
            q_g = qc * eg
            k_g = kc * jnp.exp(g_tot - g_col)
            us = []
            oq = []
            for h in range(DN_H):
                hr = slice(h * CHUNK, (h + 1) * CHUNK)
                sb = s_scr[d * DN_H + h].astype(BF16)
                ws = _bdot(jnp.concatenate([w_k[hr], q_g[hr]], axis=0), sb)
                us.append(u_t[hr] - ws[:CHUNK])
                oq.append(ws[CHUNK:])
            u_all = jnp.concatenate(us, axis=0)
            o_all = jnp.concatenate(oq, axis=0) + _bdot(qk, u_all)
            for h in range(DN_H):
                hr = slice(h * CHUNK, (h + 1) * CHUNK)
                idx = d * DN_H + h
                e_tot = jnp.exp(g_tot[h * CHUNK:h * CHUNK + 1, :])
                s_scr[idx] = e_tot * s_scr[idx] + _bdot_tn(k_g[hr], us[h])
                acc_scr[rows, h * LANES:(h + 1) * LANES] += o_all[hr]
        return carry

    lax.fori_loop(0, n_chunks, body, 0)
    sfin_ref[0] = s_scr[...]

    def post(i, carry):
        rows = pl.ds(pl.multiple_of(i * CHUNK, CHUNK), CHUNK)
        for h in range(DN_H):
            cols = slice(h * LANES, (h + 1) * LANES)
            o = acc_scr[rows, cols]
            on = o * lax.rsqrt(jnp.mean(o * o, axis=-1, keepdims=True) + RMS_EPS) * nw_ref[...]
            z = u_ref[rows, 3 * DN_W + h * LANES:3 * DN_W + (h + 1) * LANES]
            o_ref[rows, cols] = on * _silu(z)
        return carry

    lax.fori_loop(0, n_chunks, post, 0)


def deltanet_mixer(u_dn, row0, b, t, conv_w, a_log, dt_bias, norm_w, s0):
    nh2 = 2 * DN_H
    blk0 = row0 // t
    cw = jnp.pad(conv_w, ((0, SUBLANES - CONV_K), (0, 0)))
    gp = jnp.zeros((SUBLANES, LANES), F32)
    gp = gp.at[0, nh2:2 * nh2].set(a_log.reshape(-1)).at[1, nh2:2 * nh2].set(dt_bias.reshape(-1))
    st = pl.BlockSpec((1, nh2, DN_DK, DN_DV), lambda i: (i, 0, 0, 0))
    in_specs = [pl.BlockSpec((t, DN_PAD), lambda i: (blk0 + i, 0)),
                pl.BlockSpec((SUBLANES, 3 * DN_W), lambda i: (0, 0)),
                pl.BlockSpec((SUBLANES, LANES), lambda i: (0, 0)),
                pl.BlockSpec((1, DN_DV), lambda i: (0, 0))]
    args = [u_dn, cw, gp, norm_w[None, :]]
    if s0 is not None:
        in_specs.append(st)
        args.append(s0.reshape(b, nh2, DN_DK, DN_DV))
    o, sfin = pl.pallas_call(
        functools.partial(_dn_kernel, t, s0 is not None),
        grid=(b,),
        in_specs=in_specs,
        out_specs=[pl.BlockSpec((t, DN_W), lambda i: (i, 0)), st],
        out_shape=[jax.ShapeDtypeStruct((b * t, DN_W), F32),
                   jax.ShapeDtypeStruct((b, nh2, DN_DK, DN_DV), F32)],
        scratch_shapes=[pltpu.VMEM((t + 2 * HALO, 3 * DN_W), F32),
                        pltpu.VMEM((t, DN_W), F32), pltpu.VMEM((t, DN_W), F32), pltpu.VMEM((t, DN_W), F32),
                        pltpu.VMEM((t, LANES), F32), pltpu.VMEM((t, DN_W), F32),
                        pltpu.VMEM((nh2, DN_DK, DN_DV), F32),
                        pltpu.VMEM((N_MASKS, PACK, PACK), F32)],
        compiler_params=_cparams(("parallel",)),
        name="deltanet",
    )(*args)
    return o, sfin.reshape(b, 2, DN_H, DN_DK, DN_DV)


def _rw_kernel(t, has_s0, u_ref, mu_ref, w2_ref, a2_ref, g2_ref, wa0_ref, pv_ref, *rest):
    if has_s0:
        s0_ref, o_ref, sfin_ref = rest[:3]
        scr = rest[3:]
    else:
        o_ref, sfin_ref = rest[:2]
        scr = rest[2:]
    pad_scr, r_scr, v_scr, kk_scr, gate_scr, bonus_scr, lw_scr, kd_scr, b_scr, y_scr, z_scr, mask_scr = scr
    n_chunks = t // CHUNK
    w = RW_W

    if has_s0:
        z_scr[...] = s0_ref[0]
    else:
        z_scr[...] = jnp.zeros_like(z_scr)
    y_scr[...] = jnp.zeros_like(y_scr)
    _chunk_masks(mask_scr)
    _fill_halo(pad_scr, u_ref, t, RW_COLS)

    hr_ = lax.broadcasted_iota(jnp.int32, (w, w), 0) // RW_DH
    hc_ = lax.broadcasted_iota(jnp.int32, (w, w), 1) // RW_DH
    seg = (hr_ == hc_).astype(F32)
    head_mask = lax.broadcasted_iota(jnp.int32, (PACK, w), 0) // CHUNK == \
        lax.broadcasted_iota(jnp.int32, (PACK, w), 1) // RW_DH

    k_k = pv_ref[0:1, :]
    k_a = pv_ref[1:2, :]
    r_k = pv_ref[2:3, :]
    ln_w = pv_ref[3:4, :]
    ln_b = pv_ref[4:5, :]

    def prep(i, carry):
        r0 = pl.multiple_of(i * CHUNK, CHUNK)
        rows = pl.ds(r0, CHUNK)
        xs = []
        for cg in range(RW_COLS // LANES):
            cols = slice(cg * LANES, (cg + 1) * LANES)
            win = pad_scr[pl.ds(r0, CHUNK + 2 * HALO), cols]
            cur = win[HALO:HALO + CHUNK, :]
            nb = 0.5 * (win[HALO - 1:HALO - 1 + CHUNK, :] + win[HALO + 1:HALO + 1 + CHUNK, :])
            xs.append(cur + mu_ref[:, cols] * (nb - cur))
        x = jnp.concatenate(xs, axis=1)
        r = x[:, :w]
        k = x[:, w:2 * w]
        v = x[:, 2 * w:3 * w]
        wd = x[:, 3 * w:3 * w + 2 * RW_W_LORA]
        ad = x[:, 3 * w + 2 * RW_W_LORA:3 * w + 2 * RW_W_LORA + 2 * RW_A_LORA]
        gd = x[:, 3 * w + 2 * RW_W_LORA + 2 * RW_A_LORA:]
        w_log = -_softplus(-(wa0_ref[0:1, :] + _bdot(jnp.tanh(wd), w2_ref[...]))) - 0.5
        al = jax.nn.sigmoid(wa0_ref[1:2, :] + _bdot(ad, a2_ref[...]))
        kk = k * k_k
        kk = kk * lax.rsqrt(_hdot(kk * kk, seg) + 1e-6)
        r_scr[rows, :] = r
        v_scr[rows, :] = v
        kk_scr[rows, :] = kk
        gate_scr[rows, :] = _bdot(jax.nn.sigmoid(gd), g2_ref[...])
        bonus_scr[rows, :] = _hdot(r * k * r_k, seg) * v
        lw_scr[rows, :] = -jnp.exp(w_log)
        for d in range(2):
            dc = slice(d * w, (d + 1) * w)
            kd_scr[rows, dc] = k * (1.0 + (al[:, dc] - 1.0) * k_a)
            b_scr[rows, dc] = kk * al[:, dc]
        return carry

    lax.fori_loop(0, n_chunks, prep, 0)

    def stack(x):
        return jnp.where(head_mask, jnp.concatenate([x] * RW_H, axis=0), 0.0)

    def body(n, carry):
        for d in range(2):
            ch = n if d == 0 else n_chunks - 1 - n
            rows = pl.ds(pl.multiple_of(ch * CHUNK, CHUNK), CHUNK)
            dc = slice(d * w, (d + 1) * w)
            last = CHUNK - 1 if d == 0 else 0
            lw = lw_scr[rows, dc]
            lc = _hdot(_cum_matrix(d), lw)
            l_tot = lc[last:last + 1, :]
            e_in = jnp.exp(lc)
            e_ex = jnp.exp(lc - lw)
            e_neg = jnp.exp(-lc)
            e_rem = jnp.exp(l_tot - lc)
            kk = kk_scr[rows, :]
            kd = kd_scr[rows, dc]
            bb = b_scr[rows, dc]
            a_t = stack(-kk * e_ex)
            r_t = stack(r_scr[rows, :] * e_in).astype(BF16)
            b_t = stack(bb * e_neg).astype(BF16)
            k_t = stack(kd * e_neg).astype(BF16)
            b_h = stack(bb * e_rem).astype(BF16)
            k_h = stack(kd * e_rem).astype(BF16)
            v_m = stack(v_scr[rows, :]).astype(BF16)
            g = _bdot_nt(jnp.concatenate([a_t.astype(BF16), r_t], axis=0),
                         jnp.concatenate([b_t, k_t], axis=0))
            strict = mask_scr[2 + d]
            incl = jnp.where(mask_scr[d] == 0.0, 1.0, 0.0)
            a_ab = g[:PACK, :PACK] * strict
            a_ak = g[:PACK, PACK:] * strict
            a_rb = g[PACK:, :PACK] * incl
            a_rk = g[PACK:, PACK:] * incl
            nm = _tri_inverse(a_ab, mask_scr)
            rhs = jnp.concatenate([_bdot(a_ak, v_m), a_t], axis=1)
            sol = rhs + _bdot(nm, rhs)
            p0 = sol[:, :w]
            a_bar = sol[:, w:]
            zt = z_scr[d]
            az = _bdot_nt(jnp.concatenate([a_bar.astype(BF16), r_t], axis=0), zt)
            p = (p0 + az[:PACK]).astype(BF16)
            y = _bdot(a_rk, v_m) + az[PACK:] + _bdot(a_rb, p)
            y_scr[rows, :] += (y[0:CHUNK] + y[CHUNK:2 * CHUNK]) + (y[2 * CHUNK:3 * CHUNK] + y[3 * CHUNK:])
            upd = _bdot_tn(jnp.concatenate([v_m, p], axis=0), jnp.concatenate([k_h, b_h], axis=0))
            z_scr[d] = zt * jnp.exp(l_tot) + upd
        return carry

    lax.fori_loop(0, n_chunks, body, 0)
    sfin_ref[0] = z_scr[...]

    def post(i, carry):
        rows = pl.ds(pl.multiple_of(i * CHUNK, CHUNK), CHUNK)
        y = y_scr[rows, :]
        m = _hdot(y, seg) * (1.0 / RW_DH)
        yc = y - m
        var = _hdot(yc * yc, seg) * (1.0 / RW_DH)
        yn = yc * lax.rsqrt(var + RW_GN_EPS) * ln_w + ln_b
        o_ref[rows, :] = (yn + bonus_scr[rows, :]) * gate_scr[rows, :]
        return carry

    lax.fori_loop(0, n_chunks, post, 0)


def _dir_blocks(m):
    z = jnp.zeros_like(m[0])
    return jnp.concatenate([jnp.concatenate([m[0], z], axis=1), jnp.concatenate([z, m[1]], axis=1)], axis=0)


def rwkv7_mixer(u_rw, row0, b, t, mu, w0, w2, a0, a2, g2, k_k, k_a, r_k, ln_w, ln_b, s0):
    w = RW_W
    blk0 = row0 // t
    w2b = _dir_blocks(w2).astype(BF16)
    a2b = _dir_blocks(a2).astype(BF16)
    wa0 = jnp.zeros((SUBLANES, 2 * w), F32).at[0].set(w0.reshape(-1)).at[1].set(a0.reshape(-1))
    pv = jnp.zeros((SUBLANES, w), F32)
    for i, p in enumerate((k_k, k_a, r_k.reshape(-1), ln_w, ln_b)):
        pv = pv.at[i].set(p)
    full = lambda a: pl.BlockSpec(a.shape, lambda i: (0,) * a.ndim)
    st = pl.BlockSpec((1, 2, w, w), lambda i: (i, 0, 0, 0))
    args = [u_rw, mu[None, :], w2b, a2b, g2.astype(BF16), wa0, pv]
    in_specs = [pl.BlockSpec((t, RW_COLS), lambda i: (blk0 + i, 0))] + [full(a) for a in args[1:]]
    eye_h = jnp.eye(RW_H, dtype=F32)
    if s0 is not None:
        in_specs.append(st)
        args.append(jnp.einsum('bdhvk,hg->bdhvgk', s0, eye_h).reshape(b, 2, w, w))
    seq = lambda width: pltpu.VMEM((t, width), F32)
    o, sfin = pl.pallas_call(
        functools.partial(_rw_kernel, t, s0 is not None),
        grid=(b,),
        in_specs=in_specs,
        out_specs=[pl.BlockSpec((t, w), lambda i: (i, 0)), st],
        out_shape=[jax.ShapeDtypeStruct((b * t, w), F32), jax.ShapeDtypeStruct((b, 2, w, w), F32)],
        scratch_shapes=[pltpu.VMEM((t + 2 * HALO, RW_COLS), F32),
                        seq(w), seq(w), seq(w), seq(w), seq(w), seq(2 * w), seq(2 * w), seq(2 * w), seq(w),
                        pltpu.VMEM((2, w, w), F32), pltpu.VMEM((N_MASKS, PACK, PACK), F32)],
        compiler_params=_cparams(("parallel",)),
        name="rwkv7",
    )(*args)
    sf = sfin.reshape(b, 2, RW_H, RW_DH, RW_H, RW_DH)
    return o, jnp.einsum('bdhvhk->bdhvk', sf)


def _gla_kernel(n_chunks, q_ref, k_ref, v_ref, gk_ref, s0_ref, o_ref, sfin_ref, s_scr):
    c = GLA_CHUNK
    s_scr[...] = s0_ref[0]
    o_ref[...] = jnp.zeros_like(o_ref)
    ii = lax.broadcasted_iota(jnp.int32, (c, c), 0)
    jj = lax.broadcasted_iota(jnp.int32, (c, c), 1)
    cum_m = ((ii >= jj).astype(F32), (ii <= jj).astype(F32))
    row_i = lax.broadcasted_iota(jnp.int32, (c, 1), 0)
    hk = lax.broadcasted_iota(jnp.int32, (GLA_KW, GLA_VW), 0) // GLA_DK
    hv = lax.broadcasted_iota(jnp.int32, (GLA_KW, GLA_VW), 1) // GLA_DV
    head_sum = (hk == hv).astype(BF16)
    bd_mask = (lax.broadcasted_iota(jnp.int32, (GLA_VW, GLA_KW), 0) // GLA_DV
               == lax.broadcasted_iota(jnp.int32, (GLA_VW, GLA_KW), 1) // GLA_DK)

    def body(n, carry):
        for d in range(2):
            ch = n if d == 0 else n_chunks - 1 - n
            r0 = pl.multiple_of(ch * c, c)
            rows = pl.ds(r0, c)
            qc = q_ref[0, rows, :]
            kc = k_ref[0, rows, :]
            vc = v_ref[0, rows, :]
            g = gk_ref[0, rows, d * GLA_KW:(d + 1) * GLA_KW]
            bc = _hdot(cum_m[d], g)
            last = c - 1 if d == 0 else 0
            bc_last = bc[last:last + 1, :]
            terms = []
            for j in range(c):
                keep = (row_i >= j) if d == 0 else (row_i <= j)
                dec = jnp.exp(jnp.where(keep, bc - bc[j:j + 1, :], NEG_BIG))
                terms.append(qc * kc[j:j + 1, :] * dec)
            t_all = jnp.concatenate(terms, axis=0).astype(BF16)
            att = jnp.dot(t_all, head_sum, preferred_element_type=F32)
            o = None
            for j in range(c):
                term = att[j * c:(j + 1) * c, :] * vc[j:j + 1, :]
                o = term if o is None else o + term
            q_g = qc * jnp.exp(bc)
            k_g = kc * jnp.exp(bc_last - bc)
            st = s_scr[d]
            o = o + _bdot_nt(q_g, st)
            upd = _bdot_tn(vc, k_g)
            s_scr[d] = st * jnp.exp(bc_last) + jnp.where(bd_mask, upd, 0.0)
            o_ref[0, rows, :] += o
        return carry

    lax.fori_loop(0, n_chunks, body, 0)
    sfin_ref[0] = s_scr[...]


def gla_core(q, k, v, gk, s0t):
    b, t, _ = q.shape
    n_chunks = t // GLA_CHUNK
    seq = lambda width: pl.BlockSpec((1, t, width), lambda i: (i, 0, 0))
    st = pl.BlockSpec((1, 2, GLA_VW, GLA_KW), lambda i: (i, 0, 0, 0))
    return pl.pallas_call(
        functools.partial(_gla_kernel, n_chunks),
        grid=(b,),
        in_specs=[seq(GLA_KW), seq(GLA_KW), seq(GLA_VW), seq(2 * GLA_KW), st],
        out_specs=[seq(GLA_VW), st],
        out_shape=[jax.ShapeDtypeStruct((b, t, GLA_VW), F32),
                   jax.ShapeDtypeStruct((b, 2, GLA_VW, GLA_KW), F32)],
        scratch_shapes=[pltpu.VMEM((2, GLA_VW, GLA_KW), F32)],
        compiler_params=_cparams(("parallel",)),
        name="gla",
    )(q, k, v, gk, s0t)


def _rmsnorm(x, g, eps=RMS_EPS):
    return x * lax.rsqrt(jnp.mean(x * x, axis=-1, keepdims=True) + eps) * g


def gla_mixer(gla_u, gk2, gk_b, norm_w, s0):
    b, t, _ = gla_u.shape
    q = gla_u[..., :GLA_KW] * GLA_DK ** -0.5
    k = gla_u[..., GLA_KW:2 * GLA_KW]
    v = gla_u[..., 2 * GLA_KW:2 * GLA_KW + GLA_VW]
    g = gla_u[..., 2 * GLA_KW + GLA_VW:2 * GLA_KW + 2 * GLA_VW]
    gkd = gla_u[..., 2 * GLA_KW + 2 * GLA_VW:2 * GLA_KW + 2 * GLA_VW + 2 * GLA_GK_LORA]
    gkd = gkd.reshape(b, t, 2, GLA_GK_LORA)
    gk = jax.nn.log_sigmoid(jnp.einsum('btdr,drc->btdc', gkd, gk2) + gk_b) / GLA_GATE_NORM
    gk = gk.reshape(b, t, 2 * GLA_KW)
    eye_h = jnp.eye(GLA_H, dtype=F32)
    s0t = jnp.einsum('bdhkv,hg->bdhvgk', s0, eye_h).reshape(b, 2, GLA_VW, GLA_KW)
    o, sfin_t = gla_core(q, k, v, gk, s0t)
    sf = sfin_t.reshape(b, 2, GLA_H, GLA_DV, GLA_H, GLA_DK)
    sfin = jnp.einsum('bdhvhk->bdhkv', sf)
    o = _rmsnorm(o.reshape(b, t, GLA_H, GLA_DV), norm_w) * _silu(g.reshape(b, t, GLA_H, GLA_DV))
    return o.reshape(b * t, GLA_VW), sfin


def _moe_kernel(be_ref, nb_ref, x_ref, wgu_ref, bgu_ref, wdn_ref, bdn_ref, y_ref, wgu_bf, wdn_bf):
    i = pl.program_id(0)
    used = i < nb_ref[0]
    e = be_ref[i]
    prev = be_ref[jnp.maximum(i - 1, 0)]
    fresh = jnp.logical_or(i == 0, e != prev)

    @pl.when(jnp.logical_and(used, fresh))
    def _():
        wgu_bf[...] = wgu_ref[0].astype(BF16)
        wdn_bf[...] = wdn_ref[0].astype(BF16)

    @pl.when(used)
    def _():
        gu = jnp.dot(x_ref[...], wgu_bf[...], preferred_element_type=F32) + bgu_ref[0]
        gt = jnp.minimum(gu[:, :D_FF], SWIGLU_LIMIT)
        up = jnp.clip(gu[:, D_FF:], -SWIGLU_LIMIT, SWIGLU_LIMIT)
        act = (up + 1.0) * (gt * jax.nn.sigmoid(gt * SWIGLU_ALPHA))
        y_ref[...] = jnp.dot(act.astype(BF16), wdn_bf[...], preferred_element_type=F32) + bdn_ref[0]

    @pl.when(jnp.logical_not(used))
    def _():
        y_ref[...] = jnp.zeros_like(y_ref)


def moe_experts(block_e, n_used, buf, w_gu, b_gu, w_dn, b_dn):
    n_rows = buf.shape[0]
    n_blocks = n_rows // MOE_BM
    grid_spec = pltpu.PrefetchScalarGridSpec(
        num_scalar_prefetch=2,
        grid=(n_blocks,),
        in_specs=[
            pl.BlockSpec((MOE_BM, D_MODEL), lambda i, be, nb: (i, 0)),
            pl.BlockSpec((1, D_MODEL, 2 * D_FF), lambda i, be, nb: (be[i], 0, 0)),
            pl.BlockSpec((1, 1, 2 * D_FF), lambda i, be, nb: (be[i], 0, 0)),
            pl.BlockSpec((1, D_FF, D_MODEL), lambda i, be, nb: (be[i], 0, 0)),
            pl.BlockSpec((1, 1, D_MODEL), lambda i, be, nb: (be[i], 0, 0)),
        ],
        out_specs=pl.BlockSpec((MOE_BM, D_MODEL), lambda i, be, nb: (i, 0)),
        scratch_shapes=[pltpu.VMEM((D_MODEL, 2 * D_FF), BF16), pltpu.VMEM((D_FF, D_MODEL), BF16)],
    )
    return pl.pallas_call(
        _moe_kernel,
        grid_spec=grid_spec,
        out_shape=jax.ShapeDtypeStruct((n_rows, D_MODEL), F32),
        compiler_params=_cparams(("arbitrary",)),
        name="moe_experts",
    )(block_e, n_used, buf, w_gu, b_gu.reshape(N_EXPERTS, 1, 2 * D_FF), w_dn, b_dn.reshape(N_EXPERTS, 1, D_MODEL))


def moe_ffn(h2, logits, w_gu, b_gu, w_dn, b_dn):
    n_tok = h2.shape[0]
    n_assign = n_tok * TOP_K
    n_blocks = n_assign // MOE_BM + N_EXPERTS
    top_val, top_idx = lax.top_k(logits[:, :N_EXPERTS], TOP_K)
    gates = jax.nn.softmax(top_val, axis=-1)
    e_flat = top_idx.reshape(-1).astype(jnp.int32)
    ar = jnp.arange(n_assign, dtype=jnp.int32)
    e_sorted, order = lax.sort((e_flat, ar), num_keys=1, is_stable=True)
    counts = jnp.sum(e_flat[:, None] == jnp.arange(N_EXPERTS, dtype=jnp.int32)[None, :], axis=0, dtype=jnp.int32)
    padded = (counts + MOE_BM - 1) // MOE_BM * MOE_BM
    pad_end = jnp.cumsum(padded)
    pad_start = pad_end - padded
    start = jnp.cumsum(counts) - counts
    dest_sorted = pad_start[e_sorted] + ar - start[e_sorted]
    _, pos = lax.sort((order, dest_sorted), num_keys=1)
    blk_start = jnp.arange(n_blocks, dtype=jnp.int32) * MOE_BM
    n_used = (pad_end[-1] // MOE_BM).astype(jnp.int32)
    block_e = jnp.sum(pad_end[None, :] <= blk_start[:, None], axis=1, dtype=jnp.int32)
    block_e = jnp.minimum(block_e, N_EXPERTS - 1)
    block_e = jnp.where(jnp.arange(n_blocks) < n_used, block_e, block_e[jnp.maximum(n_used - 1, 0)])
    row = jnp.arange(n_blocks * MOE_BM, dtype=jnp.int32)
    row_e = block_e[row // MOE_BM]
    local = row - pad_start[row_e]
    valid = local < counts[row_e]
    src = order[jnp.clip(start[row_e] + local, 0, n_assign - 1)] // TOP_K
    buf = jnp.where(valid[:, None], h2[src], jnp.zeros((), h2.dtype))
    y_buf = moe_experts(block_e, n_used.reshape(1), buf, w_gu, b_gu, w_dn, b_dn)
    y_sel = y_buf[pos].reshape(n_tok, TOP_K, D_MODEL)
    return jnp.sum(y_sel * gates[:, :, None], axis=1)


def grid_pos_embed(rows, dim):
    t = jnp.arange(rows * GRID_W)
    r = (t // GRID_W).astype(F32)
    col = (t % GRID_W).astype(F32)
    quarter = dim // 4
    omega = 1.0 / (POS_THETA ** (jnp.arange(quarter, dtype=F32) / quarter))
    er = r[:, None] * omega
    ec = col[:, None] * omega
    return jnp.concatenate([jnp.sin(er), jnp.cos(er), jnp.sin(ec), jnp.cos(ec)], axis=-1)


def _pad_cols(w_in):
    z = lambda n: jnp.zeros(w_in.shape[:-1] + (n,), w_in.dtype)
    return jnp.concatenate([
        w_in[..., :DN_COLS], z(DN_PAD - DN_COLS),
        w_in[..., DN_COLS:DN_COLS + RW_COLS],
        w_in[..., DN_COLS + RW_COLS:], z(GLA_PAD - GLA_COLS)], axis=-1)


def kernel(x_prompt, x_sample, state_delta, state_rwkv, state_gla, c, c_ctx, w_mod, b_mod, norm_mix, norm_ffn, norm_out, w_in, w_out, dn_conv, dn_a_log, dn_dt_bias, dn_norm, rw_mu, rw_w0, rw_w2, rw_a0, rw_a2, rw_g2, rw_k_k, rw_k_a, rw_r_k, rw_ln_w, rw_ln_b, gla_gk2, gla_gk_b, gla_norm, router_w, router_b, moe_w_gu, moe_b_gu, moe_w_dn, moe_b_dn):
    bp, tp, d = x_prompt.shape
    bs, ts, _ = x_sample.shape
    depth = w_in.shape[0]
    n_ctx = bp * tp
    n_lat = bs * ts
    tiles = dict(n_ctx_tiles=n_ctx // ROW_TILE, tiles_per_latent=ts // ROW_TILE)

    cond = jnp.concatenate([c_ctx[None, :], c, jnp.zeros((SUBLANES - 1 - bs, d), F32)], axis=0)
    mod_all = modulation(cond, w_mod, b_mod).reshape(depth, SUBLANES, N_MOD, d)
    mod_all = jnp.pad(mod_all, ((0, 0), (0, 0), (0, SUBLANES - N_MOD), (0, 0)))

    w_in_p = _pad_cols(w_in).astype(BF16)
    w_out_b = w_out.astype(BF16)
    router_w_p = jnp.pad(router_w, ((0, 0), (0, 0), (0, LANES - N_EXPERTS)))
    router_b_p = jnp.pad(router_b, ((0, 0), (0, LANES - N_EXPERTS)))[:, None, :]

    pos = grid_pos_embed(ts // GRID_W, d)
    x = jnp.concatenate([x_prompt.reshape(n_ctx, d), x_sample.reshape(n_lat, d)], axis=0)
    delta = jnp.concatenate([jnp.zeros((n_ctx, d), F32), jnp.tile(pos, (bs, 1))], axis=0)

    z_gla = jnp.zeros((bp, 2, GLA_H, GLA_DK, GLA_DV), F32)
    st_dn, st_rw, st_gla = [], [], []
    for l in range(depth):
        mod = mod_all[l]
        x, u_dn, u_rw, u_gla = inproj(x, delta, mod_all[max(l - 1, 0)], mod, norm_mix[l][None, :], w_in_p[l],
                                      gate_row=None if l == 0 else 5, **tiles)
        o_dn, o_rw, o_gla = [], [], []
        for (lo, b, t, s_dn, s_rw, s_gla) in (
                (0, bp, tp, None, None, z_gla),
                (n_ctx, bs, ts, state_delta[:, l], state_rwkv[:, l], state_gla[:, l])):
            od, f_dn = deltanet_mixer(u_dn, lo, b, t, dn_conv[l], dn_a_log[l], dn_dt_bias[l], dn_norm[l], s_dn)
            orw, f_rw = rwkv7_mixer(u_rw, lo, b, t, rw_mu[l], rw_w0[l], rw_w2[l], rw_a0[l], rw_a2[l], rw_g2[l],
                                    rw_k_k[l], rw_k_a[l], rw_r_k[l], rw_ln_w[l], rw_ln_b[l], s_rw)
            og, f_gla = gla_mixer(u_gla[lo:lo + b * t].reshape(b, t, GLA_PAD), gla_gk2[l], gla_gk_b[l],
                                  gla_norm[l], s_gla)
            o_dn.append(od)
            o_rw.append(orw)
            o_gla.append(og)
            if lo == 0:
                st_dn.append(f_dn)
                st_rw.append(f_rw)
                st_gla.append(f_gla)
        x, h2, logits = outproj(jnp.concatenate(o_dn, axis=0), jnp.concatenate(o_rw, axis=0),
                                jnp.concatenate(o_gla, axis=0), x, mod, norm_ffn[l][None, :], w_out_b[l],
                                router_w_p[l], router_b_p[l], **tiles)
        delta = moe_ffn(h2, logits, moe_w_gu[l], moe_b_gu[l], moe_w_dn[l], moe_b_dn[l])
    y = final_norm(x, delta, mod_all[depth - 1], norm_out[None, :], **tiles)
    y_prompt = y[:n_ctx].reshape(bp, tp, d)
    y_sample = y[n_ctx:].reshape(bs, ts, d)
    return (y_prompt, y_sample, jnp.stack(st_dn, axis=1), jnp.stack(st_rw, axis=1), jnp.stack(st_gla, axis=1))
```

```python
import functools

import jax
import jax.numpy as jnp
from jax import lax
from jax.experimental import pallas as pl
from jax.experimental.pallas import tpu as pltpu

F32 = jnp.float32
BF16 = jnp.bfloat16
HIGHEST = lax.Precision.HIGHEST

D_MODEL = 1024
GRID_W = 64
POS_THETA = 10000.0
N_MOD = 6
RMS_EPS = 1e-6

DN_H, DN_DK, DN_DV = 4, 128, 128
DN_W = DN_H * DN_DV
CONV_K = 5

RW_H, RW_DH = 4, 64
RW_W = RW_H * RW_DH
RW_W_LORA, RW_A_LORA, RW_G_LORA = 64, 64, 128
RW_GN_EPS = 64e-5

GLA_H, GLA_DK, GLA_DV = 4, 32, 64
GLA_KW = GLA_H * GLA_DK
GLA_VW = GLA_H * GLA_DV
GLA_GK_LORA = 16
GLA_GATE_NORM = 16.0
GLA_CHUNK = 16

MIX_W = DN_W + RW_W + GLA_VW
DN_COLS = 4 * DN_W + 4 * DN_H
RW_COLS = 3 * RW_W + 2 * RW_W_LORA + 2 * RW_A_LORA + RW_G_LORA
GLA_COLS = 2 * GLA_KW + 2 * GLA_VW + 2 * GLA_GK_LORA

N_EXPERTS = 32
TOP_K = 4
D_FF = 1024
SWIGLU_LIMIT = 7.0
SWIGLU_ALPHA = 1.702

LANES = 128
SUBLANES = 8
VMEM_LIMIT = 56 * 1024 * 1024

DN_PAD = 17 * LANES
RW_PAD = RW_COLS
GLA_PAD = 7 * LANES
U_COLS = DN_PAD + RW_PAD + GLA_PAD

ROW_TILE = 256
MOE_BM = 256
CHUNK = 64
PACK = DN_H * CHUNK
GLA_CHUNKS_PER_STEP = 4
CHUNKS_PER_STEP = 2
N_MERGE = CHUNK.bit_length() - 1
N_MASKS = 4 + N_MERGE
HALO = SUBLANES
NEG_BIG = -1e30


def _cparams(sem):
    return pltpu.CompilerParams(dimension_semantics=sem, vmem_limit_bytes=VMEM_LIMIT)


def _silu(x):
    return x * jax.nn.sigmoid(x)


def _softplus(x):
    return jnp.maximum(x, 0.0) + jnp.log(1.0 + jnp.exp(-jnp.abs(x)))


def _bdot(a, b):
    return jnp.dot(a.astype(BF16), b.astype(BF16), preferred_element_type=F32)


def _bdot_nt(a, b):
    return lax.dot_general(a.astype(BF16), b.astype(BF16), (((1,), (1,)), ((), ())),
                           preferred_element_type=F32)


def _bdot_tn(a, b):
    return lax.dot_general(a.astype(BF16), b.astype(BF16), (((0,), (0,)), ((), ())),
                           preferred_element_type=F32)


def _hdot(a, b):
    return jnp.dot(a, b, precision=HIGHEST, preferred_element_type=F32)


def _mod_kernel(c_ref, w_ref, b_ref, o_ref):
    o_ref[0] = _hdot(_silu(c_ref[...]), w_ref[0]) + b_ref[0]


def modulation(cond, w_mod, b_mod):
    depth = w_mod.shape[0]
    n_out = w_mod.shape[2]
    tn = 1536
    return pl.pallas_call(
        _mod_kernel,
        grid=(depth, n_out // tn),
        in_specs=[
            pl.BlockSpec((SUBLANES, D_MODEL), lambda l, j: (0, 0)),
            pl.BlockSpec((1, D_MODEL, tn), lambda l, j: (l, 0, j)),
            pl.BlockSpec((1, 1, tn), lambda l, j: (l, 0, j)),
        ],
        out_specs=pl.BlockSpec((1, SUBLANES, tn), lambda l, j: (l, 0, j)),
        out_shape=jax.ShapeDtypeStruct((depth, SUBLANES, n_out), F32),
        compiler_params=_cparams(("arbitrary", "arbitrary")),
        name="modulation",
    )(cond, w_mod, b_mod.reshape(depth, 1, n_out))


def _tile_cond(i, n_ctx_tiles, tiles_per_latent):
    return jnp.where(i < n_ctx_tiles, 0, 1 + (i - n_ctx_tiles) // tiles_per_latent)


def _inproj_kernel(gate_row, x_ref, d_ref, mg_ref, mod_ref, g_ref, w_ref, xo_ref, udn_ref, urw_ref, ugla_ref):
    x = x_ref[...]
    if gate_row is None:
        x = x + d_ref[...]
    else:
        x = x + mg_ref[0, gate_row:gate_row + 1, :] * d_ref[...]
    xo_ref[...] = x
    y = x * lax.rsqrt(jnp.mean(x * x, axis=-1, keepdims=True) + RMS_EPS) * g_ref[...]
    h = (y * (1.0 + mod_ref[0, 1:2, :]) + mod_ref[0, 0:1, :]).astype(BF16)
    udn_ref[...] = jnp.dot(h, w_ref[:, :DN_PAD], preferred_element_type=F32)
    urw_ref[...] = jnp.dot(h, w_ref[:, DN_PAD:DN_PAD + RW_PAD], preferred_element_type=F32)
    ugla_ref[...] = jnp.dot(h, w_ref[:, DN_PAD + RW_PAD:], preferred_element_type=F32)


def inproj(x, delta, mod_gate, mod, g, w, n_ctx_tiles, tiles_per_latent, gate_row):
    n = x.shape[0]
    cond_of = functools.partial(_tile_cond, n_ctx_tiles=n_ctx_tiles, tiles_per_latent=tiles_per_latent)
    rows = lambda width: pl.BlockSpec((ROW_TILE, width), lambda i: (i, 0))
    modspec = pl.BlockSpec((1, SUBLANES, D_MODEL), lambda i: (cond_of(i), 0, 0))
    return pl.pallas_call(
        functools.partial(_inproj_kernel, gate_row),
        grid=(n // ROW_TILE,),
        in_specs=[rows(D_MODEL), rows(D_MODEL), modspec, modspec,
                  pl.BlockSpec((1, D_MODEL), lambda i: (0, 0)),
                  pl.BlockSpec((D_MODEL, U_COLS), lambda i: (0, 0))],
        out_specs=[rows(D_MODEL), rows(DN_PAD), rows(RW_PAD), rows(GLA_PAD)],
        out_shape=[jax.ShapeDtypeStruct((n, D_MODEL), F32),
                   jax.ShapeDtypeStruct((n, DN_PAD), F32),
                   jax.ShapeDtypeStruct((n, RW_PAD), F32),
                   jax.ShapeDtypeStruct((n, GLA_PAD), F32)],
        compiler_params=_cparams(("parallel",)),
        name="inproj",
    )(x, delta, mod_gate, mod, g, w)


def _outproj_kernel(odn_ref, orw_ref, ogla_ref, x_ref, mod_ref, g_ref, w_ref, rw_ref, rb_ref,
                    x2_ref, h2_ref, lg_ref):
    mix = (jnp.dot(odn_ref[...].astype(BF16), w_ref[:DN_W, :], preferred_element_type=F32)
           + jnp.dot(orw_ref[...].astype(BF16), w_ref[DN_W:DN_W + RW_W, :], preferred_element_type=F32)
           + jnp.dot(ogla_ref[...].astype(BF16), w_ref[DN_W + RW_W:, :], preferred_element_type=F32))
    x2 = x_ref[...] + mod_ref[0, 2:3, :] * mix
    x2_ref[...] = x2
    y = x2 * lax.rsqrt(jnp.mean(x2 * x2, axis=-1, keepdims=True) + RMS_EPS) * g_ref[...]
    h2 = y * (1.0 + mod_ref[0, 4:5, :]) + mod_ref[0, 3:4, :]
    h2_ref[...] = h2.astype(BF16)
    lg_ref[...] = _hdot(h2, rw_ref[...]) + rb_ref[...]


def outproj(o_dn, o_rw, o_gla, x, mod, g, w, router_w, router_b, n_ctx_tiles, tiles_per_latent):
    n = x.shape[0]
    cond_of = functools.partial(_tile_cond, n_ctx_tiles=n_ctx_tiles, tiles_per_latent=tiles_per_latent)
    rows = lambda width: pl.BlockSpec((ROW_TILE, width), lambda i: (i, 0))
    return pl.pallas_call(
        _outproj_kernel,
        grid=(n // ROW_TILE,),
        in_specs=[rows(DN_W), rows(RW_W), rows(GLA_VW), rows(D_MODEL),
                  pl.BlockSpec((1, SUBLANES, D_MODEL), lambda i: (cond_of(i), 0, 0)),
                  pl.BlockSpec((1, D_MODEL), lambda i: (0, 0)),
                  pl.BlockSpec((MIX_W, D_MODEL), lambda i: (0, 0)),
                  pl.BlockSpec((D_MODEL, LANES), lambda i: (0, 0)),
                  pl.BlockSpec((1, LANES), lambda i: (0, 0))],
        out_specs=[rows(D_MODEL), rows(D_MODEL), rows(LANES)],
        out_shape=[jax.ShapeDtypeStruct((n, D_MODEL), F32),
                   jax.ShapeDtypeStruct((n, D_MODEL), BF16),
                   jax.ShapeDtypeStruct((n, LANES), F32)],
        compiler_params=_cparams(("parallel",)),
        name="outproj",
    )(o_dn, o_rw, o_gla, x, mod, g, w, router_w, router_b)


def _final_kernel(x_ref, d_ref, mod_ref, g_ref, y_ref):
    x = x_ref[...] + mod_ref[0, 5:6, :] * d_ref[...]
    y_ref[...] = x * lax.rsqrt(jnp.mean(x * x, axis=-1, keepdims=True) + RMS_EPS) * g_ref[...]


def final_norm(x, delta, mod, g, n_ctx_tiles, tiles_per_latent):
    n = x.shape[0]
    cond_of = functools.partial(_tile_cond, n_ctx_tiles=n_ctx_tiles, tiles_per_latent=tiles_per_latent)
    rows = pl.BlockSpec((ROW_TILE, D_MODEL), lambda i: (i, 0))
    return pl.pallas_call(
        _final_kernel,
        grid=(n // ROW_TILE,),
        in_specs=[rows, rows,
                  pl.BlockSpec((1, SUBLANES, D_MODEL), lambda i: (cond_of(i), 0, 0)),
                  pl.BlockSpec((1, D_MODEL), lambda i: (0, 0))],
        out_specs=rows,
        out_shape=jax.ShapeDtypeStruct((n, D_MODEL), F32),
        compiler_params=_cparams(("parallel",)),
        name="final_norm",
    )(x, delta, mod, g)


def _chunk_masks(mask_scr):
    r = lax.broadcasted_iota(jnp.int32, (PACK, PACK), 0)
    c = lax.broadcasted_iota(jnp.int32, (PACK, PACK), 1)
    same = (r // CHUNK) == (c // CHUNK)
    tr = r % CHUNK
    tc = c % CHUNK
    mask_scr[0] = jnp.where(same & (tc <= tr), 0.0, NEG_BIG)
    mask_scr[1] = jnp.where(same & (tc >= tr), 0.0, NEG_BIG)
    mask_scr[2] = (same & (tc < tr)).astype(F32)
    mask_scr[3] = (same & (tc > tr)).astype(F32)
    for i in range(N_MERGE):
        m = 1 << i
        mask_scr[4 + i] = (((r // (2 * m)) == (c // (2 * m))) & ((r // m) != (c // m))).astype(F32)


def _cum_matrix(d):
    i = lax.broadcasted_iota(jnp.int32, (CHUNK, CHUNK), 0)
    j = lax.broadcasted_iota(jnp.int32, (CHUNK, CHUNK), 1)
    return ((j <= i) if d == 0 else (j >= i)).astype(F32)


def _tri_inverse(a, mask_scr):
    n = a * mask_scr[4]
    for i in range(1, N_MERGE):
        c = a * mask_scr[4 + i]
        x = c + _bdot(c, n)
        yield
        n = n + x + _bdot(n, x)
        yield
    return n


def _interleave(gens):
    results = [None] * len(gens)
    active = list(range(len(gens)))
    while active:
        for i in list(active):
            try:
                next(gens[i])
            except StopIteration as stop:
                results[i] = stop.value
                active.remove(i)
    return results


def _fill_halo(pad_scr, src_ref, t, width):
    zeros = jnp.zeros((HALO, width), F32)
    pad_scr[0:HALO, :] = zeros
    pad_scr[HALO + t:2 * HALO + t, :] = zeros

    def copy(i, carry):
        r0 = pl.multiple_of(i * CHUNK, CHUNK)
        pad_scr[pl.ds(HALO + r0, CHUNK), :] = src_ref[pl.ds(r0, CHUNK), :width]
        return carry

    lax.fori_loop(0, t // CHUNK, copy, 0)


def _dn_kernel(t, has_s0, u_ref, cw_ref, gp_ref, nw_ref, *rest):
    if has_s0:
        s0_ref, o_ref, sfin_ref = rest[:3]
        scr = rest[3:]
    else:
        o_ref, sfin_ref = rest[:2]
        scr = rest[2:]
    pad_scr, q_scr, k_scr, v_scr, bg_scr, acc_scr, s_scr, mask_scr = scr
    n_chunks = t // CHUNK
    qkv_w = 3 * DN_W

    if has_s0:
        s_scr[...] = s0_ref[0]
    else:
        s_scr[...] = jnp.zeros_like(s_scr)
    acc_scr[...] = jnp.zeros_like(acc_scr)
    _chunk_masks(mask_scr)
    _fill_halo(pad_scr, u_ref, t, qkv_w)

    lane = lax.broadcasted_iota(jnp.int32, (1, LANES), 1)
    neg_a = -jnp.exp(gp_ref[0:1, :])
    dt_b = gp_ref[1:2, :]

    def prep(i, carry):
        r0 = pl.multiple_of(i * CHUNK, CHUNK)
        rows = pl.ds(r0, CHUNK)
        for cg in range(qkv_w // LANES):
            cols = slice(cg * LANES, (cg + 1) * LANES)
            win = pad_scr[pl.ds(r0, CHUNK + 2 * HALO), cols]
            acc = None
            for j in range(CONV_K):
                off = HALO - CONV_K // 2 + j
                term = cw_ref[j:j + 1, cols] * win[off:off + CHUNK, :]
                acc = term if acc is None else acc + term
            y = _silu(acc)
            which, hc = divmod(cg, DN_H)
            hcols = slice(hc * LANES, (hc + 1) * LANES)
            if which < 2:
                y = y * lax.rsqrt(jnp.sum(y * y, axis=-1, keepdims=True) + 1e-6)
            if which == 0:
                q_scr[rows, hcols] = y * DN_DK ** -0.5
            elif which == 1:
                k_scr[rows, hcols] = y
            else:
                v_scr[rows, hcols] = y
        ub = u_ref[rows, 4 * DN_W:4 * DN_W + LANES]
        bg_scr[rows, :] = jnp.where(lane < 2 * DN_H, jax.nn.sigmoid(ub), neg_a * _softplus(ub + dt_b))
        return carry

    lax.fori_loop(0, n_chunks, prep, 0)

    def stack(ref, rows):
        return jnp.concatenate([ref[rows, h * LANES:(h + 1) * LANES] for h in range(DN_H)], axis=0)

    def local(d, rows):
        last = CHUNK - 1 if d == 0 else 0
        bg = bg_scr[rows, :]
        g_cum = _hdot(_cum_matrix(d), bg)
        lanes_g = [2 * DN_H + d * DN_H + h for h in range(DN_H)]
        beta = jnp.concatenate([bg[:, d * DN_H + h:d * DN_H + h + 1] for h in range(DN_H)], axis=0)
        g_col = jnp.concatenate([g_cum[:, l:l + 1] for l in lanes_g], axis=0)
        g_tot = jnp.concatenate(
            [jnp.broadcast_to(g_cum[last:last + 1, l:l + 1], (CHUNK, 1)) for l in lanes_g], axis=0)
        g_row = jnp.transpose(jnp.broadcast_to(g_col, (PACK, LANES)))[0:1, :]
        decay = jnp.exp(g_col - g_row + mask_scr[d])
        qc = stack(q_scr, rows)
        kc = stack(k_scr, rows)
        vc = stack(v_scr, rows)
        qkk = _bdot_nt(jnp.concatenate([qc, kc], axis=0), kc)
        yield
        qk = (qkk[:PACK] * decay).astype(BF16)
        low = (beta * qkk[PACK:]) * (decay * mask_scr[2 + d])
        nm = yield from _tri_inverse(-low, mask_scr)
        eg = jnp.exp(g_col)
        rhs = jnp.concatenate([beta * vc, (beta * eg) * kc], axis=1)
        sol = rhs + _bdot(nm, rhs)
        yield
        wq = [jnp.concatenate([sol[h * CHUNK:(h + 1) * CHUNK, DN_DV:],
                               (qc * eg)[h * CHUNK:(h + 1) * CHUNK]], axis=0).astype(BF16)
              for h in range(DN_H)]
        k_g = (kc * jnp.exp(g_tot - g_col)).astype(BF16)
        return sol[:, :DN_DV], wq, qk, k_g, jnp.exp(g_tot)

    def carried(d, chunk_rows, chunk_parts):
        for rows, parts in zip(chunk_rows, chunk_parts):
            u_t, wq, qk, k_g, e_tot = parts
            us = []
            oq = []
            for h in range(DN_H):
                hr = slice(h * CHUNK, (h + 1) * CHUNK)
                ws = _bdot(wq[h], s_scr[d * DN_H + h])
                us.append(u_t[hr] - ws[:CHUNK])
                oq.append(ws[CHUNK:])
            yield
            u_all = jnp.concatenate(us, axis=0).astype(BF16)
            o_all = jnp.concatenate(oq, axis=0) + _bdot(qk, u_all)
            for h in range(DN_H):
                hr = slice(h * CHUNK, (h + 1) * CHUNK)
                idx = d * DN_H + h
                s_scr[idx] = e_tot[h * CHUNK:h * CHUNK + 1, :] * s_scr[idx] + _bdot_tn(k_g[hr], u_all[hr])
                acc_scr[rows, h * LANES:(h + 1) * LANES] += o_all[hr]
            yield

    def body(n, carry):
        rows = [[], []]
        for j in range(CHUNKS_PER_STEP):
            for d in range(2):
                ch = n * CHUNKS_PER_STEP + j
                ch = ch if d == 0 else n_chunks - 1 - ch
                rows[d].append(pl.ds(pl.multiple_of(ch * CHUNK, CHUNK), CHUNK))
        flat = [(d, r) for d in range(2) for r in rows[d]]
        parts = _interleave([local(d, r) for d, r in flat])
        by_dir = [[p for (dd, _), p in zip(flat, parts) if dd == d] for d in range(2)]
        _interleave([carried(d, rows[d], by_dir[d]) for d in range(2)])
        return carry

    lax.fori_loop(0, n_chunks // CHUNKS_PER_STEP, body, 0)
    sfin_ref[0] = s_scr[...]

    def post(i, carry):
        rows = pl.ds(pl.multiple_of(i * CHUNK, CHUNK), CHUNK)
        for h in range(DN_H):
            cols = slice(h * LANES, (h + 1) * LANES)
            o = acc_scr[rows, cols]
            on = o * lax.rsqrt(jnp.mean(o * o, axis=-1, keepdims=True) + RMS_EPS) * nw_ref[...]
            z = u_ref[rows, 3 * DN_W + h * LANES:3 * DN_W + (h + 1) * LANES]
            o_ref[rows, cols] = on * _silu(z)
        return carry

    lax.fori_loop(0, n_chunks, post, 0)


def deltanet_mixer(u_dn, row0, b, t, conv_w, a_log, dt_bias, norm_w, s0):
    nh2 = 2 * DN_H
    blk0 = row0 // t
    cw = jnp.pad(conv_w, ((0, SUBLANES - CONV_K), (0, 0)))
    gp = jnp.zeros((SUBLANES, LANES), F32)
    gp = gp.at[0, nh2:2 * nh2].set(a_log.reshape(-1)).at[1, nh2:2 * nh2].set(dt_bias.reshape(-1))
    st = pl.BlockSpec((1, nh2, DN_DK, DN_DV), lambda i: (i, 0, 0, 0))
    in_specs = [pl.BlockSpec((t, DN_PAD), lambda i: (blk0 + i, 0)),
                pl.BlockSpec((SUBLANES, 3 * DN_W), lambda i: (0, 0)),
                pl.BlockSpec((SUBLANES, LANES), lambda i: (0, 0)),
                pl.BlockSpec((1, DN_DV), lambda i: (0, 0))]
    args = [u_dn, cw, gp, norm_w[None, :]]
    if s0 is not None:
        in_specs.append(st)
        args.append(s0.reshape(b, nh2, DN_DK, DN_DV))
    o, sfin = pl.pallas_call(
        functools.partial(_dn_kernel, t, s0 is not None),
        grid=(b,),
        in_specs=in_specs,
        out_specs=[pl.BlockSpec((t, DN_W), lambda i: (i, 0)), st],
        out_shape=[jax.ShapeDtypeStruct((b * t, DN_W), F32),
                   jax.ShapeDtypeStruct((b, nh2, DN_DK, DN_DV), F32)],
        scratch_shapes=[pltpu.VMEM((t + 2 * HALO, 3 * DN_W), F32),
                        pltpu.VMEM((t, DN_W), F32), pltpu.VMEM((t, DN_W), F32), pltpu.VMEM((t, DN_W), F32),
                        pltpu.VMEM((t, LANES), F32), pltpu.VMEM((t, DN_W), F32),
                        pltpu.VMEM((nh2, DN_DK, DN_DV), F32),
                        pltpu.VMEM((N_MASKS, PACK, PACK), F32)],
        compiler_params=_cparams(("parallel",)),
        name="deltanet",
    )(*args)
    return o, sfin.reshape(b, 2, DN_H, DN_DK, DN_DV)


def _rw_kernel(t, has_s0, u_ref, mu_ref, w2_ref, a2_ref, g2_ref, wa0_ref, pv_ref, *rest):
    if has_s0:
        s0_ref, o_ref, sfin_ref = rest[:3]
        scr = rest[3:]
    else:
        o_ref, sfin_ref = rest[:2]
        scr = rest[2:]
    pad_scr, r_scr, v_scr, kk_scr, gate_scr, bonus_scr, lw_scr, kd_scr, b_scr, y_scr, z_scr, mask_scr = scr
    n_chunks = t // CHUNK
    w = RW_W

    if has_s0:
        z_scr[...] = s0_ref[0]
    else:
        z_scr[...] = jnp.zeros_like(z_scr)
    y_scr[...] = jnp.zeros_like(y_scr)
    _chunk_masks(mask_scr)
    _fill_halo(pad_scr, u_ref, t, RW_COLS)

    hr_ = lax.broadcasted_iota(jnp.int32, (w, w), 0) // RW_DH
    hc_ = lax.broadcasted_iota(jnp.int32, (w, w), 1) // RW_DH
    seg = (hr_ == hc_).astype(F32)
    head_mask = lax.broadcasted_iota(jnp.int32, (PACK, w), 0) // CHUNK == \
        lax.broadcasted_iota(jnp.int32, (PACK, w), 1) // RW_DH

    k_k = pv_ref[0:1, :]
    k_a = pv_ref[1:2, :]
    r_k = pv_ref[2:3, :]
    ln_w = pv_ref[3:4, :]
    ln_b = pv_ref[4:5, :]

    def prep(i, carry):
        r0 = pl.multiple_of(i * CHUNK, CHUNK)
        rows = pl.ds(r0, CHUNK)
        xs = []
        for cg in range(RW_COLS // LANES):
            cols = slice(cg * LANES, (cg + 1) * LANES)
            win = pad_scr[pl.ds(r0, CHUNK + 2 * HALO), cols]
            cur = win[HALO:HALO + CHUNK, :]
            nb = 0.5 * (win[HALO - 1:HALO - 1 + CHUNK, :] + win[HALO + 1:HALO + 1 + CHUNK, :])
            xs.append(cur + mu_ref[:, cols] * (nb - cur))
        x = jnp.concatenate(xs, axis=1)
        r = x[:, :w]
        k = x[:, w:2 * w]
        v = x[:, 2 * w:3 * w]
        wd = x[:, 3 * w:3 * w + 2 * RW_W_LORA]
        ad = x[:, 3 * w + 2 * RW_W_LORA:3 * w + 2 * RW_W_LORA + 2 * RW_A_LORA]
        gd = x[:, 3 * w + 2 * RW_W_LORA + 2 * RW_A_LORA:]
        w_log = -_softplus(-(wa0_ref[0:1, :] + _bdot(jnp.tanh(wd), w2_ref[...]))) - 0.5
        al = jax.nn.sigmoid(wa0_ref[1:2, :] + _bdot(ad, a2_ref[...]))
        kk = k * k_k
        kk = kk * lax.rsqrt(_hdot(kk * kk, seg) + 1e-6)
        r_scr[rows, :] = r
        v_scr[rows, :] = v
        kk_scr[rows, :] = kk
        gate_scr[rows, :] = _bdot(jax.nn.sigmoid(gd), g2_ref[...])
        bonus_scr[rows, :] = _hdot(r * k * r_k, seg) * v
        lw_scr[rows, :] = -jnp.exp(w_log)
        for d in range(2):
            dc = slice(d * w, (d + 1) * w)
            kd_scr[rows, dc] = k * (1.0 + (al[:, dc] - 1.0) * k_a)
            b_scr[rows, dc] = kk * al[:, dc]
        return carry

    lax.fori_loop(0, n_chunks, prep, 0)

    def stack(x):
        return jnp.where(head_mask, jnp.concatenate([x] * RW_H, axis=0), 0.0)

    def local(d, rows):
        dc = slice(d * w, (d + 1) * w)
        last = CHUNK - 1 if d == 0 else 0
        lw = lw_scr[rows, dc]
        lc = _hdot(_cum_matrix(d), lw)
        l_tot = lc[last:last + 1, :]
        e_in = jnp.exp(lc)
        e_ex = jnp.exp(lc - lw)
        e_neg = jnp.exp(-lc)
        e_rem = jnp.exp(l_tot - lc)
        kk = kk_scr[rows, :]
        kd = kd_scr[rows, dc]
        bb = b_scr[rows, dc]
        a_t = stack(-kk * e_ex)
        r_t = stack(r_scr[rows, :] * e_in).astype(BF16)
        b_t = stack(bb * e_neg).astype(BF16)
        k_t = stack(kd * e_neg).astype(BF16)
        kb_h = jnp.concatenate([stack(kd * e_rem), stack(bb * e_rem)], axis=0).astype(BF16)
        v_m = stack(v_scr[rows, :]).astype(BF16)
        g = _bdot_nt(jnp.concatenate([a_t.astype(BF16), r_t], axis=0),
                     jnp.concatenate([b_t, k_t], axis=0))
        yield
        strict = mask_scr[2 + d]
        incl = jnp.where(mask_scr[d] == 0.0, 1.0, 0.0)
        a_ab = g[:PACK, :PACK] * strict
        a_ak = (g[:PACK, PACK:] * strict).astype(BF16)
        a_rb = (g[PACK:, :PACK] * incl).astype(BF16)
        a_rk = (g[PACK:, PACK:] * incl).astype(BF16)
        nm = yield from _tri_inverse(a_ab, mask_scr)
        rhs = jnp.concatenate([_bdot(a_ak, v_m), a_t], axis=1)
        yield
        sol = rhs + _bdot(nm, rhs)
        yield
        ar = jnp.concatenate([sol[:, w:].astype(BF16), r_t], axis=0)
        y0 = _bdot(a_rk, v_m)
        yield
        return sol[:, :w], ar, y0, a_rb, v_m, kb_h, jnp.exp(l_tot)

    def carried(d, chunk_rows, chunk_parts):
        for rows, parts in zip(chunk_rows, chunk_parts):
            p0, ar, y0, a_rb, v_m, kb_h, e_tot = parts
            zt = z_scr[d]
            az = _bdot_nt(ar, zt)
            yield
            p = (p0 + az[:PACK]).astype(BF16)
            y = y0 + az[PACK:] + _bdot(a_rb, p)
            y_scr[rows, :] += (y[0:CHUNK] + y[CHUNK:2 * CHUNK]) + (y[2 * CHUNK:3 * CHUNK] + y[3 * CHUNK:])
            yield
            z_scr[d] = zt * e_tot + _bdot_tn(jnp.concatenate([v_m, p], axis=0), kb_h)
            yield

    def body(n, carry):
        rows = [[], []]
        for j in range(CHUNKS_PER_STEP):
            for d in range(2):
                ch = n * CHUNKS_PER_STEP + j
                ch = ch if d == 0 else n_chunks - 1 - ch
                rows[d].append(pl.ds(pl.multiple_of(ch * CHUNK, CHUNK), CHUNK))
        flat = [(d, r) for d in range(2) for r in rows[d]]
        parts = _interleave([local(d, r) for d, r in flat])
        by_dir = [[p for (dd, _), p in zip(flat, parts) if dd == d] for d in range(2)]
        _interleave([carried(d, rows[d], by_dir[d]) for d in range(2)])
        return carry

    lax.fori_loop(0, n_chunks // CHUNKS_PER_STEP, body, 0)
    sfin_ref[0] = z_scr[...]

    def post(i, carry):
        rows = pl.ds(pl.multiple_of(i * CHUNK, CHUNK), CHUNK)
        y = y_scr[rows, :]
        m = _hdot(y, seg) * (1.0 / RW_DH)
        yc = y - m
        var = _hdot(yc * yc, seg) * (1.0 / RW_DH)
        yn = yc * lax.rsqrt(var + RW_GN_EPS) * ln_w + ln_b
        o_ref[rows, :] = (yn + bonus_scr[rows, :]) * gate_scr[rows, :]
        return carry

    lax.fori_loop(0, n_chunks, post, 0)


def _dir_blocks(m):
    z = jnp.zeros_like(m[0])
    return jnp.concatenate([jnp.concatenate([m[0], z], axis=1), jnp.concatenate([z, m[1]], axis=1)], axis=0)


def rwkv7_mixer(u_rw, row0, b, t, mu, w0, w2, a0, a2, g2, k_k, k_a, r_k, ln_w, ln_b, s0):
    w = RW_W
    blk0 = row0 // t
    w2b = _dir_blocks(w2).astype(BF16)
    a2b = _dir_blocks(a2).astype(BF16)
    wa0 = jnp.zeros((SUBLANES, 2 * w), F32).at[0].set(w0.reshape(-1)).at[1].set(a0.reshape(-1))
    pv = jnp.zeros((SUBLANES, w), F32)
    for i, p in enumerate((k_k, k_a, r_k.reshape(-1), ln_w, ln_b)):
        pv = pv.at[i].set(p)
    full = lambda a: pl.BlockSpec(a.shape, lambda i: (0,) * a.ndim)
    st = pl.BlockSpec((1, 2, w, w), lambda i: (i, 0, 0, 0))
    args = [u_rw, mu[None, :], w2b, a2b, g2.astype(BF16), wa0, pv]
    in_specs = [pl.BlockSpec((t, RW_COLS), lambda i: (blk0 + i, 0))] + [full(a) for a in args[1:]]
    eye_h = jnp.eye(RW_H, dtype=F32)
    if s0 is not None:
        in_specs.append(st)
        args.append(jnp.einsum('bdhvk,hg->bdhvgk', s0, eye_h).reshape(b, 2, w, w))
    seq = lambda width: pltpu.VMEM((t, width), F32)
    o, sfin = pl.pallas_call(
        functools.partial(_rw_kernel, t, s0 is not None),
        grid=(b,),
        in_specs=in_specs,
        out_specs=[pl.BlockSpec((t, w), lambda i: (i, 0)), st],
        out_shape=[jax.ShapeDtypeStruct((b * t, w), F32), jax.ShapeDtypeStruct((b, 2, w, w), F32)],
        scratch_shapes=[pltpu.VMEM((t + 2 * HALO, RW_COLS), F32),
                        seq(w), seq(w), seq(w), seq(w), seq(w), seq(2 * w), seq(2 * w), seq(2 * w), seq(w),
                        pltpu.VMEM((2, w, w), F32), pltpu.VMEM((N_MASKS, PACK, PACK), F32)],
        compiler_params=_cparams(("parallel",)),
        name="rwkv7",
    )(*args)
    sf = sfin.reshape(b, 2, RW_H, RW_DH, RW_H, RW_DH)
    return o, jnp.einsum('bdhvhk->bdhvk', sf)


def _gla_kernel(n_chunks, q_ref, k_ref, v_ref, gk_ref, s0_ref, o_ref, sfin_ref, s_scr):
    c = GLA_CHUNK
    s_scr[...] = s0_ref[0]
    o_ref[...] = jnp.zeros_like(o_ref)
    ii = lax.broadcasted_iota(jnp.int32, (c, c), 0)
    jj = lax.broadcasted_iota(jnp.int32, (c, c), 1)
    cum_m = ((ii >= jj).astype(F32), (ii <= jj).astype(F32))
    row_i = lax.broadcasted_iota(jnp.int32, (c, 1), 0)
    hk = lax.broadcasted_iota(jnp.int32, (GLA_KW, GLA_VW), 0) // GLA_DK
    hv = lax.broadcasted_iota(jnp.int32, (GLA_KW, GLA_VW), 1) // GLA_DV
    head_sum = (hk == hv).astype(BF16)
    bd_mask = (lax.broadcasted_iota(jnp.int32, (GLA_VW, GLA_KW), 0) // GLA_DV
               == lax.broadcasted_iota(jnp.int32, (GLA_VW, GLA_KW), 1) // GLA_DK)

    def local(d, rows):
        qc = q_ref[0, rows, :]
        kc = k_ref[0, rows, :]
        vc = v_ref[0, rows, :]
        g = gk_ref[0, rows, d * GLA_KW:(d + 1) * GLA_KW]
        bc = _hdot(cum_m[d], g)
        yield
        last = c - 1 if d == 0 else 0
        bc_last = bc[last:last + 1, :]
        terms = []
        for j in range(c):
            keep = (row_i >= j) if d == 0 else (row_i <= j)
            dec = jnp.exp(jnp.where(keep, bc - bc[j:j + 1, :], NEG_BIG))
            terms.append(qc * kc[j:j + 1, :] * dec)
        t_all = jnp.concatenate(terms, axis=0).astype(BF16)
        att = jnp.dot(t_all, head_sum, preferred_element_type=F32)
        yield
        o = None
        for j in range(c):
            term = att[j * c:(j + 1) * c, :] * vc[j:j + 1, :]
            o = term if o is None else o + term
        q_g = (qc * jnp.exp(bc)).astype(BF16)
        upd = jnp.where(bd_mask, _bdot_tn(vc, kc * jnp.exp(bc_last - bc)), 0.0)
        yield
        return o, q_g, upd, jnp.exp(bc_last)

    def carried(d, chunk_rows, chunk_parts):
        for rows, parts in zip(chunk_rows, chunk_parts):
            o, q_g, upd, e_last = parts
            st = s_scr[d]
            o_ref[0, rows, :] += o + _bdot_nt(q_g, st)
            s_scr[d] = st * e_last + upd
            yield

    def body(n, carry):
        rows = [[], []]
        for j in range(GLA_CHUNKS_PER_STEP):
            for d in range(2):
                ch = n * GLA_CHUNKS_PER_STEP + j
                ch = ch if d == 0 else n_chunks - 1 - ch
                rows[d].append(pl.ds(pl.multiple_of(ch * c, c), c))
        flat = [(d, r) for d in range(2) for r in rows[d]]
        parts = _interleave([local(d, r) for d, r in flat])
        by_dir = [[p for (dd, _), p in zip(flat, parts) if dd == d] for d in range(2)]
        _interleave([carried(d, rows[d], by_dir[d]) for d in range(2)])
        return carry

    lax.fori_loop(0, n_chunks // GLA_CHUNKS_PER_STEP, body, 0)
    sfin_ref[0] = s_scr[...]


def gla_core(q, k, v, gk, s0t):
    b, t, _ = q.shape
    n_chunks = t // GLA_CHUNK
    seq = lambda width: pl.BlockSpec((1, t, width), lambda i: (i, 0, 0))
    st = pl.BlockSpec((1, 2, GLA_VW, GLA_KW), lambda i: (i, 0, 0, 0))
    return pl.pallas_call(
        functools.partial(_gla_kernel, n_chunks),
        grid=(b,),
        in_specs=[seq(GLA_KW), seq(GLA_KW), seq(GLA_VW), seq(2 * GLA_KW), st],
        out_specs=[seq(GLA_VW), st],
        out_shape=[jax.ShapeDtypeStruct((b, t, GLA_VW), F32),
                   jax.ShapeDtypeStruct((b, 2, GLA_VW, GLA_KW), F32)],
        scratch_shapes=[pltpu.VMEM((2, GLA_VW, GLA_KW), F32)],
        compiler_params=_cparams(("parallel",)),
        name="gla",
    )(q, k, v, gk, s0t)


def _rmsnorm(x, g, eps=RMS_EPS):
    return x * lax.rsqrt(jnp.mean(x * x, axis=-1, keepdims=True) + eps) * g


def gla_mixer(gla_u, gk2, gk_b, norm_w, s0):
    b, t, _ = gla_u.shape
    q = gla_u[..., :GLA_KW] * GLA_DK ** -0.5
    k = gla_u[..., GLA_KW:2 * GLA_KW]
    v = gla_u[..., 2 * GLA_KW:2 * GLA_KW + GLA_VW]
    g = gla_u[..., 2 * GLA_KW + GLA_VW:2 * GLA_KW + 2 * GLA_VW]
    gkd = gla_u[..., 2 * GLA_KW + 2 * GLA_VW:2 * GLA_KW + 2 * GLA_VW + 2 * GLA_GK_LORA]
    gkd = gkd.reshape(b, t, 2, GLA_GK_LORA)
    gk = jax.nn.log_sigmoid(jnp.einsum('btdr,drc->btdc', gkd, gk2) + gk_b) / GLA_GATE_NORM
    gk = gk.reshape(b, t, 2 * GLA_KW)
    eye_h = jnp.eye(GLA_H, dtype=F32)
    s0t = jnp.einsum('bdhkv,hg->bdhvgk', s0, eye_h).reshape(b, 2, GLA_VW, GLA_KW)
    o, sfin_t = gla_core(q, k, v, gk, s0t)
    sf = sfin_t.reshape(b, 2, GLA_H, GLA_DV, GLA_H, GLA_DK)
    sfin = jnp.einsum('bdhvhk->bdhkv', sf)
    o = _rmsnorm(o.reshape(b, t, GLA_H, GLA_DV), norm_w) * _silu(g.reshape(b, t, GLA_H, GLA_DV))
    return o.reshape(b * t, GLA_VW), sfin


def _moe_kernel(be_ref, nb_ref, x_ref, wgu_ref, bgu_ref, wdn_ref, bdn_ref, y_ref, wgu_bf, wdn_bf):
    i = pl.program_id(0)
    used = i < nb_ref[0]
    e = be_ref[i]
    prev = be_ref[jnp.maximum(i - 1, 0)]
    fresh = jnp.logical_or(i == 0, e != prev)

    @pl.when(jnp.logical_and(used, fresh))
    def _():
        wgu_bf[...] = wgu_ref[0, 0].astype(BF16)
        wdn_bf[...] = wdn_ref[0, 0].astype(BF16)

    @pl.when(used)
    def _():
        gu = jnp.dot(x_ref[...], wgu_bf[...], preferred_element_type=F32) + bgu_ref[0, 0]
        gt = jnp.minimum(gu[:, :D_FF], SWIGLU_LIMIT)
        up = jnp.clip(gu[:, D_FF:], -SWIGLU_LIMIT, SWIGLU_LIMIT)
        act = (up + 1.0) * (gt * jax.nn.sigmoid(gt * SWIGLU_ALPHA))
        y_ref[...] = jnp.dot(act.astype(BF16), wdn_bf[...], preferred_element_type=F32) + bdn_ref[0, 0]

    @pl.when(jnp.logical_not(used))
    def _():
        y_ref[...] = jnp.zeros_like(y_ref)


def moe_experts(layer, block_e, n_used, buf, w_gu, b_gu, w_dn, b_dn):
    n_rows = buf.shape[0]
    n_blocks = n_rows // MOE_BM
    depth = w_gu.shape[0]
    grid_spec = pltpu.PrefetchScalarGridSpec(
        num_scalar_prefetch=2,
        grid=(n_blocks,),
        in_specs=[
            pl.BlockSpec((MOE_BM, D_MODEL), lambda i, be, nb: (i, 0)),
            pl.BlockSpec((1, 1, D_MODEL, 2 * D_FF), lambda i, be, nb: (layer, be[i], 0, 0)),
            pl.BlockSpec((1, 1, 1, 2 * D_FF), lambda i, be, nb: (layer, be[i], 0, 0)),
            pl.BlockSpec((1, 1, D_FF, D_MODEL), lambda i, be, nb: (layer, be[i], 0, 0)),
            pl.BlockSpec((1, 1, 1, D_MODEL), lambda i, be, nb: (layer, be[i], 0, 0)),
        ],
        out_specs=pl.BlockSpec((MOE_BM, D_MODEL), lambda i, be, nb: (i, 0)),
        scratch_shapes=[pltpu.VMEM((D_MODEL, 2 * D_FF), BF16), pltpu.VMEM((D_FF, D_MODEL), BF16)],
    )
    return pl.pallas_call(
        _moe_kernel,
        grid_spec=grid_spec,
        out_shape=jax.ShapeDtypeStruct((n_rows, D_MODEL), F32),
        compiler_params=_cparams(("arbitrary",)),
        name="moe_experts",
    )(block_e, n_used, buf, w_gu, b_gu.reshape(depth, N_EXPERTS, 1, 2 * D_FF),
      w_dn, b_dn.reshape(depth, N_EXPERTS, 1, D_MODEL))


def moe_ffn(layer, h2, logits, w_gu, b_gu, w_dn, b_dn):
    n_tok = h2.shape[0]
    n_assign = n_tok * TOP_K
    n_blocks = n_assign // MOE_BM + N_EXPERTS
    top_val, top_idx = lax.top_k(logits[:, :N_EXPERTS], TOP_K)
    gates = jax.nn.softmax(top_val, axis=-1)
    e_flat = top_idx.reshape(-1).astype(jnp.int32)
    ar = jnp.arange(n_assign, dtype=jnp.int32)
    e_sorted, order = lax.sort((e_flat, ar), num_keys=1, is_stable=True)
    counts = jnp.sum(e_flat[:, None] == jnp.arange(N_EXPERTS, dtype=jnp.int32)[None, :], axis=0, dtype=jnp.int32)
    padded = (counts + MOE_BM - 1) // MOE_BM * MOE_BM
    pad_end = jnp.cumsum(padded)
    pad_start = pad_end - padded
    start = jnp.cumsum(counts) - counts
    dest_sorted = pad_start[e_sorted] + ar - start[e_sorted]
    _, pos = lax.sort((order, dest_sorted), num_keys=1)
    blk_start = jnp.arange(n_blocks, dtype=jnp.int32) * MOE_BM
    n_used = (pad_end[-1] // MOE_BM).astype(jnp.int32)
    block_e = jnp.sum(pad_end[None, :] <= blk_start[:, None], axis=1, dtype=jnp.int32)
    block_e = jnp.minimum(block_e, N_EXPERTS - 1)
    block_e = jnp.where(jnp.arange(n_blocks) < n_used, block_e, block_e[jnp.maximum(n_used - 1, 0)])
    row = jnp.arange(n_blocks * MOE_BM, dtype=jnp.int32)
    row_e = block_e[row // MOE_BM]
    local = row - pad_start[row_e]
    valid = local < counts[row_e]
    src = order[jnp.clip(start[row_e] + local, 0, n_assign - 1)] // TOP_K
    buf = jnp.where(valid[:, None], h2[src], jnp.zeros((), h2.dtype))
    y_buf = moe_experts(layer, block_e, n_used.reshape(1), buf, w_gu, b_gu, w_dn, b_dn)
    pos_k = pos.reshape(n_tok, TOP_K)
    out = gates[:, 0:1] * y_buf[pos_k[:, 0]]
    for j in range(1, TOP_K):
        out = out + gates[:, j:j + 1] * y_buf[pos_k[:, j]]
    return out


def grid_pos_embed(rows, dim):
    t = jnp.arange(rows * GRID_W)
    r = (t // GRID_W).astype(F32)
    col = (t % GRID_W).astype(F32)
    quarter = dim // 4
    omega = 1.0 / (POS_THETA ** (jnp.arange(quarter, dtype=F32) / quarter))
    er = r[:, None] * omega
    ec = col[:, None] * omega
    return jnp.concatenate([jnp.sin(er), jnp.cos(er), jnp.sin(ec), jnp.cos(ec)], axis=-1)


def _pad_cols(w_in):
    z = lambda n: jnp.zeros(w_in.shape[:-1] + (n,), w_in.dtype)
    return jnp.concatenate([
        w_in[..., :DN_COLS], z(DN_PAD - DN_COLS),
        w_in[..., DN_COLS:DN_COLS + RW_COLS],
        w_in[..., DN_COLS + RW_COLS:], z(GLA_PAD - GLA_COLS)], axis=-1)


def kernel(x_prompt, x_sample, state_delta, state_rwkv, state_gla, c, c_ctx, w_mod, b_mod, norm_mix, norm_ffn, norm_out, w_in, w_out, dn_conv, dn_a_log, dn_dt_bias, dn_norm, rw_mu, rw_w0, rw_w2, rw_a0, rw_a2, rw_g2, rw_k_k, rw_k_a, rw_r_k, rw_ln_w, rw_ln_b, gla_gk2, gla_gk_b, gla_norm, router_w, router_b, moe_w_gu, moe_b_gu, moe_w_dn, moe_b_dn):
    bp, tp, d = x_prompt.shape
    bs, ts, _ = x_sample.shape
    depth = w_in.shape[0]
    n_ctx = bp * tp
    n_lat = bs * ts
    tiles = dict(n_ctx_tiles=n_ctx // ROW_TILE, tiles_per_latent=ts // ROW_TILE)

    cond = jnp.concatenate([c_ctx[None, :], c, jnp.zeros((SUBLANES - 1 - bs, d), F32)], axis=0)
    mod_all = modulation(cond, w_mod, b_mod).reshape(depth, SUBLANES, N_MOD, d)
    mod_all = jnp.pad(mod_all, ((0, 0), (0, 0), (0, SUBLANES - N_MOD), (0, 0)))

    w_in_p = _pad_cols(w_in).astype(BF16)
    w_out_b = w_out.astype(BF16)
    router_w_p = jnp.pad(router_w, ((0, 0), (0, 0), (0, LANES - N_EXPERTS)))
    router_b_p = jnp.pad(router_b, ((0, 0), (0, LANES - N_EXPERTS)))[:, None, :]

    pos = grid_pos_embed(ts // GRID_W, d)
    x = jnp.concatenate([x_prompt.reshape(n_ctx, d), x_sample.reshape(n_lat, d)], axis=0)
    delta = jnp.concatenate([jnp.zeros((n_ctx, d), F32), jnp.tile(pos, (bs, 1))], axis=0)

    z_gla = jnp.zeros((bp, 2, GLA_H, GLA_DK, GLA_DV), F32)
    st_dn, st_rw, st_gla = [], [], []
    for l in range(depth):
        mod = mod_all[l]
        x, u_dn, u_rw, u_gla = inproj(x, delta, mod_all[max(l - 1, 0)], mod, norm_mix[l][None, :], w_in_p[l],
                                      gate_row=None if l == 0 else 5, **tiles)
        o_dn, o_rw, o_gla = [], [], []
        for (lo, b, t, s_dn, s_rw, s_gla) in (
                (0, bp, tp, None, None, z_gla),
                (n_ctx, bs, ts, state_delta[:, l], state_rwkv[:, l], state_gla[:, l])):
            od, f_dn = deltanet_mixer(u_dn, lo, b, t, dn_conv[l], dn_a_log[l], dn_dt_bias[l], dn_norm[l], s_dn)
            orw, f_rw = rwkv7_mixer(u_rw, lo, b, t, rw_mu[l], rw_w0[l], rw_w2[l], rw_a0[l], rw_a2[l], rw_g2[l],
                                    rw_k_k[l], rw_k_a[l], rw_r_k[l], rw_ln_w[l], rw_ln_b[l], s_rw)
            og, f_gla = gla_mixer(u_gla[lo:lo + b * t].reshape(b, t, GLA_PAD), gla_gk2[l], gla_gk_b[l],
                                  gla_norm[l], s_gla)
            o_dn.append(od)
            o_rw.append(orw)
            o_gla.append(og)
            if lo == 0:
                st_dn.append(f_dn)
                st_rw.append(f_rw)
                st_gla.append(f_gla)
        x, h2, logits = outproj(jnp.concatenate(o_dn, axis=0), jnp.concatenate(o_rw, axis=0),
                                jnp.concatenate(o_gla, axis=0), x, mod, norm_ffn[l][None, :], w_out_b[l],
                                router_w_p[l], router_b_p[l], **tiles)
        delta = moe_ffn(l, h2, logits, moe_w_gu, moe_b_gu, moe_w_dn, moe_b_dn)
    y = final_norm(x, delta, mod_all[depth - 1], norm_out[None, :], **tiles)
    y_prompt = y[:n_ctx].reshape(bp, tp, d)
    y_sample = y[n_ctx:].reshape(bs, ts, d)
    return (y_prompt, y_sample, jnp.stack(st_dn, axis=1), jnp.stack(st_rw, axis=1), jnp.stack(st_gla, axis=1))
```

```python
import functools

import jax
import jax.numpy as jnp
from jax import lax
from jax.experimental import pallas as pl
from jax.experimental.pallas import tpu as pltpu

F32 = jnp.float32
BF16 = jnp.bfloat16
HIGHEST = lax.Precision.HIGHEST

D_MODEL = 1024
GRID_W = 64
POS_THETA = 10000.0
N_MOD = 6
RMS_EPS = 1e-6

DN_H, DN_DK, DN_DV = 4, 128, 128
DN_W = DN_H * DN_DV
CONV_K = 5

RW_H, RW_DH = 4, 64
RW_W = RW_H * RW_DH
RW_W_LORA, RW_A_LORA, RW_G_LORA = 64, 64, 128
RW_GN_EPS = 64e-5

GLA_H, GLA_DK, GLA_DV = 4, 32, 64
GLA_KW = GLA_H * GLA_DK
GLA_VW = GLA_H * GLA_DV
GLA_GK_LORA = 16
GLA_GATE_NORM = 16.0
GLA_CHUNK = 16

MIX_W = DN_W + RW_W + GLA_VW
DN_COLS = 4 * DN_W + 4 * DN_H
RW_COLS = 3 * RW_W + 2 * RW_W_LORA + 2 * RW_A_LORA + RW_G_LORA
GLA_COLS = 2 * GLA_KW + 2 * GLA_VW + 2 * GLA_GK_LORA

N_EXPERTS = 32
TOP_K = 4
D_FF = 1024
SWIGLU_LIMIT = 7.0
SWIGLU_ALPHA = 1.702

LANES = 128
SUBLANES = 8
VMEM_LIMIT = 56 * 1024 * 1024

DN_PAD = 17 * LANES
RW_PAD = RW_COLS
GLA_PAD = 7 * LANES
U_COLS = DN_PAD + RW_PAD + GLA_PAD

ROW_TILE = 256
MOE_BM = 256
CHUNK = 64
PACK = DN_H * CHUNK
GLA_CHUNKS_PER_STEP = 4
CHUNKS_PER_STEP = 2
N_MERGE = CHUNK.bit_length() - 1
N_MASKS = 4 + N_MERGE
HALO = SUBLANES
NEG_BIG = -1e30


def _cparams(sem):
    return pltpu.CompilerParams(dimension_semantics=sem, vmem_limit_bytes=VMEM_LIMIT)


def _silu(x):
    return x * jax.nn.sigmoid(x)


def _softplus(x):
    return jnp.maximum(x, 0.0) + jnp.log(1.0 + jnp.exp(-jnp.abs(x)))


def _bdot(a, b):
    return jnp.dot(a.astype(BF16), b.astype(BF16), preferred_element_type=F32)


def _bdot_nt(a, b):
    return lax.dot_general(a.astype(BF16), b.astype(BF16), (((1,), (1,)), ((), ())),
                           preferred_element_type=F32)


def _bdot_tn(a, b):
    return lax.dot_general(a.astype(BF16), b.astype(BF16), (((0,), (0,)), ((), ())),
                           preferred_element_type=F32)


def _hdot(a, b):
    return jnp.dot(a, b, precision=HIGHEST, preferred_element_type=F32)


def _mod_kernel(c_ref, w_ref, b_ref, o_ref):
    o_ref[0] = _hdot(_silu(c_ref[...]), w_ref[0]) + b_ref[0]


def modulation(cond, w_mod, b_mod):
    depth = w_mod.shape[0]
    n_out = w_mod.shape[2]
    tn = 1536
    return pl.pallas_call(
        _mod_kernel,
        grid=(depth, n_out // tn),
        in_specs=[
            pl.BlockSpec((SUBLANES, D_MODEL), lambda l, j: (0, 0)),
            pl.BlockSpec((1, D_MODEL, tn), lambda l, j: (l, 0, j)),
            pl.BlockSpec((1, 1, tn), lambda l, j: (l, 0, j)),
        ],
        out_specs=pl.BlockSpec((1, SUBLANES, tn), lambda l, j: (l, 0, j)),
        out_shape=jax.ShapeDtypeStruct((depth, SUBLANES, n_out), F32),
        compiler_params=_cparams(("arbitrary", "arbitrary")),
        name="modulation",
    )(cond, w_mod, b_mod.reshape(depth, 1, n_out))


def _tile_cond(i, n_ctx_tiles, tiles_per_latent):
    return jnp.where(i < n_ctx_tiles, 0, 1 + (i - n_ctx_tiles) // tiles_per_latent)


def _inproj_kernel(gate_row, x_ref, d_ref, mg_ref, mod_ref, g_ref, w_ref, xo_ref, udn_ref, urw_ref, ugla_ref):
    x = x_ref[...]
    if gate_row is None:
        x = x + d_ref[...]
    else:
        x = x + mg_ref[0, gate_row:gate_row + 1, :] * d_ref[...]
    xo_ref[...] = x
    y = x * lax.rsqrt(jnp.mean(x * x, axis=-1, keepdims=True) + RMS_EPS) * g_ref[...]
    h = (y * (1.0 + mod_ref[0, 1:2, :]) + mod_ref[0, 0:1, :]).astype(BF16)
    udn_ref[...] = jnp.dot(h, w_ref[:, :DN_PAD], preferred_element_type=F32)
    urw_ref[...] = jnp.dot(h, w_ref[:, DN_PAD:DN_PAD + RW_PAD], preferred_element_type=F32)
    ugla_ref[...] = jnp.dot(h, w_ref[:, DN_PAD + RW_PAD:], preferred_element_type=F32)


def inproj(x, delta, mod_gate, mod, g, w, n_ctx_tiles, tiles_per_latent, gate_row):
    n = x.shape[0]
    cond_of = functools.partial(_tile_cond, n_ctx_tiles=n_ctx_tiles, tiles_per_latent=tiles_per_latent)
    rows = lambda width: pl.BlockSpec((ROW_TILE, width), lambda i: (i, 0))
    modspec = pl.BlockSpec((1, SUBLANES, D_MODEL), lambda i: (cond_of(i), 0, 0))
    return pl.pallas_call(
        functools.partial(_inproj_kernel, gate_row),
        grid=(n // ROW_TILE,),
        in_specs=[rows(D_MODEL), rows(D_MODEL), modspec, modspec,
                  pl.BlockSpec((1, D_MODEL), lambda i: (0, 0)),
                  pl.BlockSpec((D_MODEL, U_COLS), lambda i: (0, 0))],
        out_specs=[rows(D_MODEL), rows(DN_PAD), rows(RW_PAD), rows(GLA_PAD)],
        out_shape=[jax.ShapeDtypeStruct((n, D_MODEL), F32),
                   jax.ShapeDtypeStruct((n, DN_PAD), F32),
                   jax.ShapeDtypeStruct((n, RW_PAD), F32),
                   jax.ShapeDtypeStruct((n, GLA_PAD), F32)],
        compiler_params=_cparams(("parallel",)),
        name="inproj",
    )(x, delta, mod_gate, mod, g, w)


def _outproj_kernel(odn_ref, orw_ref, ogla_ref, x_ref, mod_ref, g_ref, w_ref, rw_ref, rb_ref,
                    x2_ref, h2_ref, lg_ref):
    mix = (jnp.dot(odn_ref[...].astype(BF16), w_ref[:DN_W, :], preferred_element_type=F32)
           + jnp.dot(orw_ref[...].astype(BF16), w_ref[DN_W:DN_W + RW_W, :], preferred_element_type=F32)
           + jnp.dot(ogla_ref[...].astype(BF16), w_ref[DN_W + RW_W:, :], preferred_element_type=F32))
    x2 = x_ref[...] + mod_ref[0, 2:3, :] * mix
    x2_ref[...] = x2
    y = x2 * lax.rsqrt(jnp.mean(x2 * x2, axis=-1, keepdims=True) + RMS_EPS) * g_ref[...]
    h2 = y * (1.0 + mod_ref[0, 4:5, :]) + mod_ref[0, 3:4, :]
    h2_ref[...] = h2.astype(BF16)
    lg_ref[...] = _hdot(h2, rw_ref[...]) + rb_ref[...]


def outproj(o_dn, o_rw, o_gla, x, mod, g, w, router_w, router_b, n_ctx_tiles, tiles_per_latent):
    n = x.shape[0]
    cond_of = functools.partial(_tile_cond, n_ctx_tiles=n_ctx_tiles, tiles_per_latent=tiles_per_latent)
    rows = lambda width: pl.BlockSpec((ROW_TILE, width), lambda i: (i, 0))
    return pl.pallas_call(
        _outproj_kernel,
        grid=(n // ROW_TILE,),
        in_specs=[rows(DN_W), rows(RW_W), rows(GLA_VW), rows(D_MODEL),
                  pl.BlockSpec((1, SUBLANES, D_MODEL), lambda i: (cond_of(i), 0, 0)),
                  pl.BlockSpec((1, D_MODEL), lambda i: (0, 0)),
                  pl.BlockSpec((MIX_W, D_MODEL), lambda i: (0, 0)),
                  pl.BlockSpec((D_MODEL, LANES), lambda i: (0, 0)),
                  pl.BlockSpec((1, LANES), lambda i: (0, 0))],
        out_specs=[rows(D_MODEL), rows(D_MODEL), rows(LANES)],
        out_shape=[jax.ShapeDtypeStruct((n, D_MODEL), F32),
                   jax.ShapeDtypeStruct((n, D_MODEL), BF16),
                   jax.ShapeDtypeStruct((n, LANES), F32)],
        compiler_params=_cparams(("parallel",)),
        name="outproj",
    )(o_dn, o_rw, o_gla, x, mod, g, w, router_w, router_b)


def _final_kernel(x_ref, d_ref, mod_ref, g_ref, y_ref):
    x = x_ref[...] + mod_ref[0, 5:6, :] * d_ref[...]
    y_ref[...] = x * lax.rsqrt(jnp.mean(x * x, axis=-1, keepdims=True) + RMS_EPS) * g_ref[...]


def final_norm(x, delta, mod, g, n_ctx_tiles, tiles_per_latent):
    n = x.shape[0]
    cond_of = functools.partial(_tile_cond, n_ctx_tiles=n_ctx_tiles, tiles_per_latent=tiles_per_latent)
    rows = pl.BlockSpec((ROW_TILE, D_MODEL), lambda i: (i, 0))
    return pl.pallas_call(
        _final_kernel,
        grid=(n // ROW_TILE,),
        in_specs=[rows, rows,
                  pl.BlockSpec((1, SUBLANES, D_MODEL), lambda i: (cond_of(i), 0, 0)),
                  pl.BlockSpec((1, D_MODEL), lambda i: (0, 0))],
        out_specs=rows,
        out_shape=jax.ShapeDtypeStruct((n, D_MODEL), F32),
        compiler_params=_cparams(("parallel",)),
        name="final_norm",
    )(x, delta, mod, g)


def _chunk_masks(mask_scr):
    r = lax.broadcasted_iota(jnp.int32, (PACK, PACK), 0)
    c = lax.broadcasted_iota(jnp.int32, (PACK, PACK), 1)
    same = (r // CHUNK) == (c // CHUNK)
    tr = r % CHUNK
    tc = c % CHUNK
    mask_scr[0] = jnp.where(same & (tc <= tr), 0.0, NEG_BIG)
    mask_scr[1] = jnp.where(same & (tc >= tr), 0.0, NEG_BIG)
    mask_scr[2] = (same & (tc < tr)).astype(F32)
    mask_scr[3] = (same & (tc > tr)).astype(F32)
    for i in range(N_MERGE):
        m = 1 << i
        mask_scr[4 + i] = (((r // (2 * m)) == (c // (2 * m))) & ((r // m) != (c // m))).astype(F32)


def _cum_matrix(d):
    i = lax.broadcasted_iota(jnp.int32, (CHUNK, CHUNK), 0)
    j = lax.broadcasted_iota(jnp.int32, (CHUNK, CHUNK), 1)
    return ((j <= i) if d == 0 else (j >= i)).astype(F32)


def _tri_inverse(a, mask_scr):
    n = a * mask_scr[4]
    for i in range(1, N_MERGE):
        c = a * mask_scr[4 + i]
        x = c + _bdot(c, n)
        yield
        n = n + x + _bdot(n, x)
        yield
    return n


def _interleave(gens):
    results = [None] * len(gens)
    active = list(range(len(gens)))
    while active:
        for i in list(active):
            try:
                next(gens[i])
            except StopIteration as stop:
                results[i] = stop.value
                active.remove(i)
    return results


def _fill_halo(pad_scr, src_ref, t, width):
    zeros = jnp.zeros((HALO, width), F32)
    pad_scr[0:HALO, :] = zeros
    pad_scr[HALO + t:2 * HALO + t, :] = zeros

    def copy(i, carry):
        r0 = pl.multiple_of(i * CHUNK, CHUNK)
        pad_scr[pl.ds(HALO + r0, CHUNK), :] = src_ref[pl.ds(r0, CHUNK), :width]
        return carry

    lax.fori_loop(0, t // CHUNK, copy, 0)


def _dn_kernel(t, has_s0, u_ref, cw_ref, gp_ref, nw_ref, *rest):
    if has_s0:
        s0_ref, o_ref, sfin_ref = rest[:3]
        scr = rest[3:]
    else:
        o_ref, sfin_ref = rest[:2]
        scr = rest[2:]
    pad_scr, q_scr, k_scr, v_scr, bg_scr, acc_scr, s_scr, mask_scr = scr
    n_chunks = t // CHUNK
    qkv_w = 3 * DN_W

    if has_s0:
        s_scr[...] = s0_ref[0]
    else:
        s_scr[...] = jnp.zeros_like(s_scr)
    acc_scr[...] = jnp.zeros_like(acc_scr)
    _chunk_masks(mask_scr)
    _fill_halo(pad_scr, u_ref, t, qkv_w)

    lane = lax.broadcasted_iota(jnp.int32, (1, LANES), 1)
    neg_a = -jnp.exp(gp_ref[0:1, :])
    dt_b = gp_ref[1:2, :]

    def prep(i, carry):
        r0 = pl.multiple_of(i * CHUNK, CHUNK)
        rows = pl.ds(r0, CHUNK)
        for cg in range(qkv_w // LANES):
            cols = slice(cg * LANES, (cg + 1) * LANES)
            win = pad_scr[pl.ds(r0, CHUNK + 2 * HALO), cols]
            acc = None
            for j in range(CONV_K):
                off = HALO - CONV_K // 2 + j
                term = cw_ref[j:j + 1, cols] * win[off:off + CHUNK, :]
                acc = term if acc is None else acc + term
            y = _silu(acc)
            which, hc = divmod(cg, DN_H)
            hcols = slice(hc * LANES, (hc + 1) * LANES)
            if which < 2:
                y = y * lax.rsqrt(jnp.sum(y * y, axis=-1, keepdims=True) + 1e-6)
            if which == 0:
                q_scr[rows, hcols] = y * DN_DK ** -0.5
            elif which == 1:
                k_scr[rows, hcols] = y
            else:
                v_scr[rows, hcols] = y
        ub = u_ref[rows, 4 * DN_W:4 * DN_W + LANES]
        bg_scr[rows, :] = jnp.where(lane < 2 * DN_H, jax.nn.sigmoid(ub), neg_a * _softplus(ub + dt_b))
        return carry

    lax.fori_loop(0, n_chunks, prep, 0)

    def stack(ref, rows):
        return jnp.concatenate([ref[rows, h * LANES:(h + 1) * LANES] for h in range(DN_H)], axis=0)

    def local(d, rows):
        last = CHUNK - 1 if d == 0 else 0
        bg = bg_scr[rows, :]
        g_cum = _hdot(_cum_matrix(d), bg)
        lanes_g = [2 * DN_H + d * DN_H + h for h in range(DN_H)]
        beta = jnp.concatenate([bg[:, d * DN_H + h:d * DN_H + h + 1] for h in range(DN_H)], axis=0)
        g_col = jnp.concatenate([g_cum[:, l:l + 1] for l in lanes_g], axis=0)
        g_tot = jnp.concatenate(
            [jnp.broadcast_to(g_cum[last:last + 1, l:l + 1], (CHUNK, 1)) for l in lanes_g], axis=0)
        g_row = jnp.transpose(jnp.broadcast_to(g_col, (PACK, LANES)))[0:1, :]
        decay = jnp.exp(g_col - g_row + mask_scr[d])
        qc = stack(q_scr, rows)
        kc = stack(k_scr, rows)
        vc = stack(v_scr, rows)
        qkk = _bdot_nt(jnp.concatenate([qc, kc], axis=0), kc)
        yield
        qk = (qkk[:PACK] * decay).astype(BF16)
        low = (beta * qkk[PACK:]) * (decay * mask_scr[2 + d])
        nm = yield from _tri_inverse(-low, mask_scr)
        eg = jnp.exp(g_col)
        rhs = jnp.concatenate([beta * vc, (beta * eg) * kc], axis=1)
        sol = rhs + _bdot(nm, rhs)
        yield
        wq = [jnp.concatenate([sol[h * CHUNK:(h + 1) * CHUNK, DN_DV:],
                               (qc * eg)[h * CHUNK:(h + 1) * CHUNK]], axis=0).astype(BF16)
              for h in range(DN_H)]
        k_g = (kc * jnp.exp(g_tot - g_col)).astype(BF16)
        return sol[:, :DN_DV], wq, qk, k_g, jnp.exp(g_tot)

    def carried(d, chunk_rows, chunk_parts):
        for rows, parts in zip(chunk_rows, chunk_parts):
            u_t, wq, qk, k_g, e_tot = parts
            us = []
            oq = []
            for h in range(DN_H):
                hr = slice(h * CHUNK, (h + 1) * CHUNK)
                ws = _bdot(wq[h], s_scr[d * DN_H + h])
                us.append(u_t[hr] - ws[:CHUNK])
                oq.append(ws[CHUNK:])
            yield
            u_all = jnp.concatenate(us, axis=0).astype(BF16)
            o_all = jnp.concatenate(oq, axis=0) + _bdot(qk, u_all)
            for h in range(DN_H):
                hr = slice(h * CHUNK, (h + 1) * CHUNK)
                idx = d * DN_H + h
                s_scr[idx] = e_tot[h * CHUNK:h * CHUNK + 1, :] * s_scr[idx] + _bdot_tn(k_g[hr], u_all[hr])
                acc_scr[rows, h * LANES:(h + 1) * LANES] += o_all[hr]
            yield

    def body(n, carry):
        rows = [[], []]
        for j in range(CHUNKS_PER_STEP):
            for d in range(2):
                ch = n * CHUNKS_PER_STEP + j
                ch = ch if d == 0 else n_chunks - 1 - ch
                rows[d].append(pl.ds(pl.multiple_of(ch * CHUNK, CHUNK), CHUNK))
        flat = [(d, r) for d in range(2) for r in rows[d]]
        parts = _interleave([local(d, r) for d, r in flat])
        by_dir = [[p for (dd, _), p in zip(flat, parts) if dd == d] for d in range(2)]
        _interleave([carried(d, rows[d], by_dir[d]) for d in range(2)])
        return carry

    lax.fori_loop(0, n_chunks // CHUNKS_PER_STEP, body, 0)
    sfin_ref[0] = s_scr[...]

    def post(i, carry):
        rows = pl.ds(pl.multiple_of(i * CHUNK, CHUNK), CHUNK)
        for h in range(DN_H):
            cols = slice(h * LANES, (h + 1) * LANES)
            o = acc_scr[rows, cols]
            on = o * lax.rsqrt(jnp.mean(o * o, axis=-1, keepdims=True) + RMS_EPS) * nw_ref[...]
            z = u_ref[rows, 3 * DN_W + h * LANES:3 * DN_W + (h + 1) * LANES]
            o_ref[rows, cols] = on * _silu(z)
        return carry

    lax.fori_loop(0, n_chunks, post, 0)


def deltanet_mixer(u_dn, row0, b, t, conv_w, a_log, dt_bias, norm_w, s0):
    nh2 = 2 * DN_H
    blk0 = row0 // t
    cw = jnp.pad(conv_w, ((0, SUBLANES - CONV_K), (0, 0)))
    gp = jnp.zeros((SUBLANES, LANES), F32)
    gp = gp.at[0, nh2:2 * nh2].set(a_log.reshape(-1)).at[1, nh2:2 * nh2].set(dt_bias.reshape(-1))
    st = pl.BlockSpec((1, nh2, DN_DK, DN_DV), lambda i: (i, 0, 0, 0))
    in_specs = [pl.BlockSpec((t, DN_PAD), lambda i: (blk0 + i, 0)),
                pl.BlockSpec((SUBLANES, 3 * DN_W), lambda i: (0, 0)),
                pl.BlockSpec((SUBLANES, LANES), lambda i: (0, 0)),
                pl.BlockSpec((1, DN_DV), lambda i: (0, 0))]
    args = [u_dn, cw, gp, norm_w[None, :]]
    if s0 is not None:
        in_specs.append(st)
        args.append(s0.reshape(b, nh2, DN_DK, DN_DV))
    o, sfin = pl.pallas_call(
        functools.partial(_dn_kernel, t, s0 is not None),
        grid=(b,),
        in_specs=in_specs,
        out_specs=[pl.BlockSpec((t, DN_W), lambda i: (i, 0)), st],
        out_shape=[jax.ShapeDtypeStruct((b * t, DN_W), F32),
                   jax.ShapeDtypeStruct((b, nh2, DN_DK, DN_DV), F32)],
        scratch_shapes=[pltpu.VMEM((t + 2 * HALO, 3 * DN_W), F32),
                        pltpu.VMEM((t, DN_W), F32), pltpu.VMEM((t, DN_W), F32), pltpu.VMEM((t, DN_W), F32),
                        pltpu.VMEM((t, LANES), F32), pltpu.VMEM((t, DN_W), F32),
                        pltpu.VMEM((nh2, DN_DK, DN_DV), F32),
                        pltpu.VMEM((N_MASKS, PACK, PACK), F32)],
        compiler_params=_cparams(("parallel",)),
        name="deltanet",
    )(*args)
    return o, sfin.reshape(b, 2, DN_H, DN_DK, DN_DV)


def _rw_kernel(t, has_s0, u_ref, mu_ref, w2_ref, a2_ref, g2_ref, wa0_ref, pv_ref, *rest):
    if has_s0:
        s0_ref, o_ref, sfin_ref = rest[:3]
        scr = rest[3:]
    else:
        o_ref, sfin_ref = rest[:2]
        scr = rest[2:]
    pad_scr, r_scr, v_scr, kk_scr, gate_scr, bonus_scr, lw_scr, kd_scr, b_scr, y_scr, z_scr, mask_scr = scr
    n_chunks = t // CHUNK
    w = RW_W

    if has_s0:
        z_scr[...] = s0_ref[0]
    else:
        z_scr[...] = jnp.zeros_like(z_scr)
    y_scr[...] = jnp.zeros_like(y_scr)
    _chunk_masks(mask_scr)
    _fill_halo(pad_scr, u_ref, t, RW_COLS)

    low_half = lax.broadcasted_iota(jnp.int32, (1, LANES), 1) < RW_DH

    def seg_sum(x):
        outs = []
        for gi in range(w // LANES):
            xg = x[:, gi * LANES:(gi + 1) * LANES]
            lo = jnp.sum(jnp.where(low_half, xg, 0.0), axis=-1, keepdims=True)
            hi = jnp.sum(jnp.where(low_half, 0.0, xg), axis=-1, keepdims=True)
            outs.append(jnp.where(low_half, lo, hi))
        return jnp.concatenate(outs, axis=1)

    head_mask = lax.broadcasted_iota(jnp.int32, (PACK, w), 0) // CHUNK == \
        lax.broadcasted_iota(jnp.int32, (PACK, w), 1) // RW_DH

    k_k = pv_ref[0:1, :]
    k_a = pv_ref[1:2, :]
    r_k = pv_ref[2:3, :]
    ln_w = pv_ref[3:4, :]
    ln_b = pv_ref[4:5, :]

    def prep(i, carry):
        r0 = pl.multiple_of(i * CHUNK, CHUNK)
        rows = pl.ds(r0, CHUNK)
        xs = []
        for cg in range(RW_COLS // LANES):
            cols = slice(cg * LANES, (cg + 1) * LANES)
            win = pad_scr[pl.ds(r0, CHUNK + 2 * HALO), cols]
            cur = win[HALO:HALO + CHUNK, :]
            nb = 0.5 * (win[HALO - 1:HALO - 1 + CHUNK, :] + win[HALO + 1:HALO + 1 + CHUNK, :])
            xs.append(cur + mu_ref[:, cols] * (nb - cur))
        x = jnp.concatenate(xs, axis=1)
        r = x[:, :w]
        k = x[:, w:2 * w]
        v = x[:, 2 * w:3 * w]
        wd = x[:, 3 * w:3 * w + 2 * RW_W_LORA]
        ad = x[:, 3 * w + 2 * RW_W_LORA:3 * w + 2 * RW_W_LORA + 2 * RW_A_LORA]
        gd = x[:, 3 * w + 2 * RW_W_LORA + 2 * RW_A_LORA:]
        w_log = -_softplus(-(wa0_ref[0:1, :] + _bdot(jnp.tanh(wd), w2_ref[...]))) - 0.5
        al = jax.nn.sigmoid(wa0_ref[1:2, :] + _bdot(ad, a2_ref[...]))
        kk = k * k_k
        kk = kk * lax.rsqrt(seg_sum(kk * kk) + 1e-6)
        r_scr[rows, :] = r
        v_scr[rows, :] = v
        kk_scr[rows, :] = kk
        gate_scr[rows, :] = _bdot(jax.nn.sigmoid(gd), g2_ref[...])
        bonus_scr[rows, :] = seg_sum(r * k * r_k) * v
        lw_scr[rows, :] = -jnp.exp(w_log)
        for d in range(2):
            dc = slice(d * w, (d + 1) * w)
            kd_scr[rows, dc] = k * (1.0 + (al[:, dc] - 1.0) * k_a)
            b_scr[rows, dc] = kk * al[:, dc]
        return carry

    lax.fori_loop(0, n_chunks, prep, 0)

    def stack(x):
        return jnp.where(head_mask, jnp.concatenate([x] * RW_H, axis=0), 0.0)

    def local(d, rows):
        dc = slice(d * w, (d + 1) * w)
        last = CHUNK - 1 if d == 0 else 0
        lw = lw_scr[rows, dc]
        lc = _hdot(_cum_matrix(d), lw)
        l_tot = lc[last:last + 1, :]
        e_in = jnp.exp(lc)
        e_ex = jnp.exp(lc - lw)
        e_neg = jnp.exp(-lc)
        e_rem = jnp.exp(l_tot - lc)
        kk = kk_scr[rows, :]
        kd = kd_scr[rows, dc]
        bb = b_scr[rows, dc]
        a_t = stack(-kk * e_ex)
        r_t = stack(r_scr[rows, :] * e_in).astype(BF16)
        b_t = stack(bb * e_neg).astype(BF16)
        k_t = stack(kd * e_neg).astype(BF16)
        kb_h = jnp.concatenate([stack(kd * e_rem), stack(bb * e_rem)], axis=0).astype(BF16)
        v_m = stack(v_scr[rows, :]).astype(BF16)
        g = _bdot_nt(jnp.concatenate([a_t.astype(BF16), r_t], axis=0),
                     jnp.concatenate([b_t, k_t], axis=0))
        yield
        strict = mask_scr[2 + d]
        incl = jnp.where(mask_scr[d] == 0.0, 1.0, 0.0)
        a_ab = g[:PACK, :PACK] * strict
        a_k = jnp.concatenate([g[:PACK, PACK:] * strict, g[PACK:, PACK:] * incl], axis=0).astype(BF16)
        a_rb = (g[PACK:, :PACK] * incl).astype(BF16)
        nm = yield from _tri_inverse(a_ab, mask_scr)
        av = _bdot(a_k, v_m)
        yield
        rhs = jnp.concatenate([av[:PACK], a_t], axis=1)
        sol = rhs + _bdot(nm, rhs)
        yield
        ar = jnp.concatenate([sol[:, w:].astype(BF16), r_t], axis=0)
        return sol[:, :w], ar, av[PACK:], a_rb, v_m, kb_h, jnp.exp(l_tot)

    def carried(d, chunk_rows, chunk_parts):
        for rows, parts in zip(chunk_rows, chunk_parts):
            p0, ar, y0, a_rb, v_m, kb_h, e_tot = parts
            zt = z_scr[d]
            az = _bdot_nt(ar, zt)
            yield
            p = (p0 + az[:PACK]).astype(BF16)
            y = y0 + az[PACK:] + _bdot(a_rb, p)
            y_scr[rows, :] += (y[0:CHUNK] + y[CHUNK:2 * CHUNK]) + (y[2 * CHUNK:3 * CHUNK] + y[3 * CHUNK:])
            yield
            z_scr[d] = zt * e_tot + _bdot_tn(jnp.concatenate([v_m, p], axis=0), kb_h)
            yield

    def body(n, carry):
        rows = [[], []]
        for j in range(CHUNKS_PER_STEP):
            for d in range(2):
                ch = n * CHUNKS_PER_STEP + j
                ch = ch if d == 0 else n_chunks - 1 - ch
                rows[d].append(pl.ds(pl.multiple_of(ch * CHUNK, CHUNK), CHUNK))
        flat = [(d, r) for d in range(2) for r in rows[d]]
        parts = _interleave([local(d, r) for d, r in flat])
        by_dir = [[p for (dd, _), p in zip(flat, parts) if dd == d] for d in range(2)]
        _interleave([carried(d, rows[d], by_dir[d]) for d in range(2)])
        return carry

    lax.fori_loop(0, n_chunks // CHUNKS_PER_STEP, body, 0)
    sfin_ref[0] = z_scr[...]

    def post(i, carry):
        rows = pl.ds(pl.multiple_of(i * CHUNK, CHUNK), CHUNK)
        y = y_scr[rows, :]
        m = seg_sum(y) * (1.0 / RW_DH)
        yc = y - m
        var = seg_sum(yc * yc) * (1.0 / RW_DH)
        yn = yc * lax.rsqrt(var + RW_GN_EPS) * ln_w + ln_b
        o_ref[rows, :] = (yn + bonus_scr[rows, :]) * gate_scr[rows, :]
        return carry

    lax.fori_loop(0, n_chunks, post, 0)


def _dir_blocks(m):
    z = jnp.zeros_like(m[0])
    return jnp.concatenate([jnp.concatenate([m[0], z], axis=1), jnp.concatenate([z, m[1]], axis=1)], axis=0)


def rwkv7_mixer(u_rw, row0, b, t, mu, w0, w2, a0, a2, g2, k_k, k_a, r_k, ln_w, ln_b, s0):
    w = RW_W
    blk0 = row0 // t
    w2b = _dir_blocks(w2).astype(BF16)
    a2b = _dir_blocks(a2).astype(BF16)
    wa0 = jnp.zeros((SUBLANES, 2 * w), F32).at[0].set(w0.reshape(-1)).at[1].set(a0.reshape(-1))
    pv = jnp.zeros((SUBLANES, w), F32)
    for i, p in enumerate((k_k, k_a, r_k.reshape(-1), ln_w, ln_b)):
        pv = pv.at[i].set(p)
    full = lambda a: pl.BlockSpec(a.shape, lambda i: (0,) * a.ndim)
    st = pl.BlockSpec((1, 2, w, w), lambda i: (i, 0, 0, 0))
    args = [u_rw, mu[None, :], w2b, a2b, g2.astype(BF16), wa0, pv]
    in_specs = [pl.BlockSpec((t, RW_COLS), lambda i: (blk0 + i, 0))] + [full(a) for a in args[1:]]
    eye_h = jnp.eye(RW_H, dtype=F32)
    if s0 is not None:
        in_specs.append(st)
        args.append(jnp.einsum('bdhvk,hg->bdhvgk', s0, eye_h).reshape(b, 2, w, w))
    seq = lambda width: pltpu.VMEM((t, width), F32)
    o, sfin = pl.pallas_call(
        functools.partial(_rw_kernel, t, s0 is not None),
        grid=(b,),
        in_specs=in_specs,
        out_specs=[pl.BlockSpec((t, w), lambda i: (i, 0)), st],
        out_shape=[jax.ShapeDtypeStruct((b * t, w), F32), jax.ShapeDtypeStruct((b, 2, w, w), F32)],
        scratch_shapes=[pltpu.VMEM((t + 2 * HALO, RW_COLS), F32),
                        seq(w), seq(w), seq(w), seq(w), seq(w), seq(2 * w), seq(2 * w), seq(2 * w), seq(w),
                        pltpu.VMEM((2, w, w), F32), pltpu.VMEM((N_MASKS, PACK, PACK), F32)],
        compiler_params=_cparams(("parallel",)),
        name="rwkv7",
    )(*args)
    sf = sfin.reshape(b, 2, RW_H, RW_DH, RW_H, RW_DH)
    return o, jnp.einsum('bdhvhk->bdhvk', sf)


def _gla_kernel(n_chunks, q_ref, k_ref, v_ref, gk_ref, s0_ref, o_ref, sfin_ref, s_scr):
    c = GLA_CHUNK
    s_scr[...] = s0_ref[0]
    o_ref[...] = jnp.zeros_like(o_ref)
    ii = lax.broadcasted_iota(jnp.int32, (c, c), 0)
    jj = lax.broadcasted_iota(jnp.int32, (c, c), 1)
    cum_m = ((ii >= jj).astype(F32), (ii <= jj).astype(F32))
    row_i = lax.broadcasted_iota(jnp.int32, (c, 1), 0)
    hk = lax.broadcasted_iota(jnp.int32, (GLA_KW, GLA_VW), 0) // GLA_DK
    hv = lax.broadcasted_iota(jnp.int32, (GLA_KW, GLA_VW), 1) // GLA_DV
    head_sum = (hk == hv).astype(BF16)
    bd_mask = (lax.broadcasted_iota(jnp.int32, (GLA_VW, GLA_KW), 0) // GLA_DV
               == lax.broadcasted_iota(jnp.int32, (GLA_VW, GLA_KW), 1) // GLA_DK)

    def local(d, rows):
        qc = q_ref[0, rows, :]
        kc = k_ref[0, rows, :]
        vc = v_ref[0, rows, :]
        g = gk_ref[0, rows, d * GLA_KW:(d + 1) * GLA_KW]
        bc = _hdot(cum_m[d], g)
        yield
        last = c - 1 if d == 0 else 0
        bc_last = bc[last:last + 1, :]
        terms = []
        for j in range(c):
            keep = (row_i >= j) if d == 0 else (row_i <= j)
            dec = jnp.exp(jnp.where(keep, bc - bc[j:j + 1, :], NEG_BIG))
            terms.append(qc * kc[j:j + 1, :] * dec)
        t_all = jnp.concatenate(terms, axis=0).astype(BF16)
        att = jnp.dot(t_all, head_sum, preferred_element_type=F32)
        yield
        o = None
        for j in range(c):
            term = att[j * c:(j + 1) * c, :] * vc[j:j + 1, :]
            o = term if o is None else o + term
        q_g = (qc * jnp.exp(bc)).astype(BF16)
        upd = jnp.where(bd_mask, _bdot_tn(vc, kc * jnp.exp(bc_last - bc)), 0.0)
        yield
        return o, q_g, upd, jnp.exp(bc_last)

    def carried(d, chunk_rows, chunk_parts):
        for rows, parts in zip(chunk_rows, chunk_parts):
            o, q_g, upd, e_last = parts
            st = s_scr[d]
            o_ref[0, rows, :] += o + _bdot_nt(q_g, st)
            s_scr[d] = st * e_last + upd
            yield

    def body(n, carry):
        rows = [[], []]
        for j in range(GLA_CHUNKS_PER_STEP):
            for d in range(2):
                ch = n * GLA_CHUNKS_PER_STEP + j
                ch = ch if d == 0 else n_chunks - 1 - ch
                rows[d].append(pl.ds(pl.multiple_of(ch * c, c), c))
        flat = [(d, r) for d in range(2) for r in rows[d]]
        parts = _interleave([local(d, r) for d, r in flat])
        by_dir = [[p for (dd, _), p in zip(flat, parts) if dd == d] for d in range(2)]
        _interleave([carried(d, rows[d], by_dir[d]) for d in range(2)])
        return carry

    lax.fori_loop(0, n_chunks // GLA_CHUNKS_PER_STEP, body, 0)
    sfin_ref[0] = s_scr[...]


def gla_core(q, k, v, gk, s0t):
    b, t, _ = q.shape
    n_chunks = t // GLA_CHUNK
    seq = lambda width: pl.BlockSpec((1, t, width), lambda i: (i, 0, 0))
    st = pl.BlockSpec((1, 2, GLA_VW, GLA_KW), lambda i: (i, 0, 0, 0))
    return pl.pallas_call(
        functools.partial(_gla_kernel, n_chunks),
        grid=(b,),
        in_specs=[seq(GLA_KW), seq(GLA_KW), seq(GLA_VW), seq(2 * GLA_KW), st],
        out_specs=[seq(GLA_VW), st],
        out_shape=[jax.ShapeDtypeStruct((b, t, GLA_VW), F32),
                   jax.ShapeDtypeStruct((b, 2, GLA_VW, GLA_KW), F32)],
        scratch_shapes=[pltpu.VMEM((2, GLA_VW, GLA_KW), F32)],
        compiler_params=_cparams(("parallel",)),
        name="gla",
    )(q, k, v, gk, s0t)


def _rmsnorm(x, g, eps=RMS_EPS):
    return x * lax.rsqrt(jnp.mean(x * x, axis=-1, keepdims=True) + eps) * g


def gla_mixer(gla_u, gk2, gk_b, norm_w, s0):
    b, t, _ = gla_u.shape
    q = gla_u[..., :GLA_KW] * GLA_DK ** -0.5
    k = gla_u[..., GLA_KW:2 * GLA_KW]
    v = gla_u[..., 2 * GLA_KW:2 * GLA_KW + GLA_VW]
    g = gla_u[..., 2 * GLA_KW + GLA_VW:2 * GLA_KW + 2 * GLA_VW]
    gkd = gla_u[..., 2 * GLA_KW + 2 * GLA_VW:2 * GLA_KW + 2 * GLA_VW + 2 * GLA_GK_LORA]
    gkd = gkd.reshape(b, t, 2, GLA_GK_LORA)
    gk = jax.nn.log_sigmoid(jnp.einsum('btdr,drc->btdc', gkd, gk2) + gk_b) / GLA_GATE_NORM
    gk = gk.reshape(b, t, 2 * GLA_KW)
    eye_h = jnp.eye(GLA_H, dtype=F32)
    s0t = jnp.einsum('bdhkv,hg->bdhvgk', s0, eye_h).reshape(b, 2, GLA_VW, GLA_KW)
    o, sfin_t = gla_core(q, k, v, gk, s0t)
    sf = sfin_t.reshape(b, 2, GLA_H, GLA_DV, GLA_H, GLA_DK)
    sfin = jnp.einsum('bdhvhk->bdhkv', sf)
    o = _rmsnorm(o.reshape(b, t, GLA_H, GLA_DV), norm_w) * _silu(g.reshape(b, t, GLA_H, GLA_DV))
    return o.reshape(b * t, GLA_VW), sfin


def _moe_kernel(be_ref, nb_ref, x_ref, wgu_ref, bgu_ref, wdn_ref, bdn_ref, y_ref, wgu_bf, wdn_bf):
    i = pl.program_id(0)
    used = i < nb_ref[0]
    e = be_ref[i]
    prev = be_ref[jnp.maximum(i - 1, 0)]
    fresh = jnp.logical_or(i == 0, e != prev)

    @pl.when(jnp.logical_and(used, fresh))
    def _():
        wgu_bf[...] = wgu_ref[0, 0].astype(BF16)
        wdn_bf[...] = wdn_ref[0, 0].astype(BF16)

    @pl.when(used)
    def _():
        gu = jnp.dot(x_ref[...], wgu_bf[...], preferred_element_type=F32) + bgu_ref[0, 0]
        gt = jnp.minimum(gu[:, :D_FF], SWIGLU_LIMIT)
        up = jnp.clip(gu[:, D_FF:], -SWIGLU_LIMIT, SWIGLU_LIMIT)
        act = (up + 1.0) * (gt * jax.nn.sigmoid(gt * SWIGLU_ALPHA))
        y = jnp.dot(act.astype(BF16), wdn_bf[...], preferred_element_type=F32) + bdn_ref[0, 0]
        y_ref[...] = y.astype(y_ref.dtype)

    @pl.when(jnp.logical_not(used))
    def _():
        y_ref[...] = jnp.zeros_like(y_ref)


def moe_experts(layer, block_e, n_used, buf, w_gu, b_gu, w_dn, b_dn):
    n_rows = buf.shape[0]
    n_blocks = n_rows // MOE_BM
    depth = w_gu.shape[0]
    grid_spec = pltpu.PrefetchScalarGridSpec(
        num_scalar_prefetch=2,
        grid=(n_blocks,),
        in_specs=[
            pl.BlockSpec((MOE_BM, D_MODEL), lambda i, be, nb: (i, 0)),
            pl.BlockSpec((1, 1, D_MODEL, 2 * D_FF), lambda i, be, nb: (layer, be[i], 0, 0)),
            pl.BlockSpec((1, 1, 1, 2 * D_FF), lambda i, be, nb: (layer, be[i], 0, 0)),
            pl.BlockSpec((1, 1, D_FF, D_MODEL), lambda i, be, nb: (layer, be[i], 0, 0)),
            pl.BlockSpec((1, 1, 1, D_MODEL), lambda i, be, nb: (layer, be[i], 0, 0)),
        ],
        out_specs=pl.BlockSpec((MOE_BM, D_MODEL), lambda i, be, nb: (i, 0)),
        scratch_shapes=[pltpu.VMEM((D_MODEL, 2 * D_FF), BF16), pltpu.VMEM((D_FF, D_MODEL), BF16)],
    )
    return pl.pallas_call(
        _moe_kernel,
        grid_spec=grid_spec,
        out_shape=jax.ShapeDtypeStruct((n_rows, D_MODEL), BF16),
        compiler_params=_cparams(("arbitrary",)),
        name="moe_experts",
    )(block_e, n_used, buf, w_gu, b_gu.reshape(depth, N_EXPERTS, 1, 2 * D_FF),
      w_dn, b_dn.reshape(depth, N_EXPERTS, 1, D_MODEL))


def moe_ffn(layer, h2, logits, w_gu, b_gu, w_dn, b_dn):
    n_tok = h2.shape[0]
    n_assign = n_tok * TOP_K
    n_blocks = n_assign // MOE_BM + N_EXPERTS
    top_val, top_idx = lax.top_k(logits[:, :N_EXPERTS], TOP_K)
    gates = jax.nn.softmax(top_val, axis=-1)
    e_flat = top_idx.reshape(-1).astype(jnp.int32)
    ar = jnp.arange(n_assign, dtype=jnp.int32)
    e_sorted, order = lax.sort((e_flat, ar), num_keys=1, is_stable=True)
    counts = jnp.sum(e_flat[:, None] == jnp.arange(N_EXPERTS, dtype=jnp.int32)[None, :], axis=0, dtype=jnp.int32)
    padded = (counts + MOE_BM - 1) // MOE_BM * MOE_BM
    pad_end = jnp.cumsum(padded)
    pad_start = pad_end - padded
    start = jnp.cumsum(counts) - counts
    dest_sorted = pad_start[e_sorted] + ar - start[e_sorted]
    _, pos = lax.sort((order, dest_sorted), num_keys=1)
    blk_start = jnp.arange(n_blocks, dtype=jnp.int32) * MOE_BM
    n_used = (pad_end[-1] // MOE_BM).astype(jnp.int32)
    block_e = jnp.sum(pad_end[None, :] <= blk_start[:, None], axis=1, dtype=jnp.int32)
    block_e = jnp.minimum(block_e, N_EXPERTS - 1)
    block_e = jnp.where(jnp.arange(n_blocks) < n_used, block_e, block_e[jnp.maximum(n_used - 1, 0)])
    row = jnp.arange(n_blocks * MOE_BM, dtype=jnp.int32)
    row_e = block_e[row // MOE_BM]
    src = order[jnp.clip(start[row_e] + row - pad_start[row_e], 0, n_assign - 1)] // TOP_K
    y_buf = moe_experts(layer, block_e, n_used.reshape(1), h2[src], w_gu, b_gu, w_dn, b_dn)
    pos_k = pos.reshape(n_tok, TOP_K)
    out = gates[:, 0:1] * y_buf[pos_k[:, 0]].astype(F32)
    for j in range(1, TOP_K):
        out = out + gates[:, j:j + 1] * y_buf[pos_k[:, j]].astype(F32)
    return out


def grid_pos_embed(rows, dim):
    t = jnp.arange(rows * GRID_W)
    r = (t // GRID_W).astype(F32)
    col = (t % GRID_W).astype(F32)
    quarter = dim // 4
    omega = 1.0 / (POS_THETA ** (jnp.arange(quarter, dtype=F32) / quarter))
    er = r[:, None] * omega
    ec = col[:, None] * omega
    return jnp.concatenate([jnp.sin(er), jnp.cos(er), jnp.sin(ec), jnp.cos(ec)], axis=-1)


def _pad_cols(w_in):
    z = lambda n: jnp.zeros(w_in.shape[:-1] + (n,), w_in.dtype)
    return jnp.concatenate([
        w_in[..., :DN_COLS], z(DN_PAD - DN_COLS),
        w_in[..., DN_COLS:DN_COLS + RW_COLS],
        w_in[..., DN_COLS + RW_COLS:], z(GLA_PAD - GLA_COLS)], axis=-1)


def kernel(x_prompt, x_sample, state_delta, state_rwkv, state_gla, c, c_ctx, w_mod, b_mod, norm_mix, norm_ffn, norm_out, w_in, w_out, dn_conv, dn_a_log, dn_dt_bias, dn_norm, rw_mu, rw_w0, rw_w2, rw_a0, rw_a2, rw_g2, rw_k_k, rw_k_a, rw_r_k, rw_ln_w, rw_ln_b, gla_gk2, gla_gk_b, gla_norm, router_w, router_b, moe_w_gu, moe_b_gu, moe_w_dn, moe_b_dn):
    bp, tp, d = x_prompt.shape
    bs, ts, _ = x_sample.shape
    depth = w_in.shape[0]
    n_ctx = bp * tp
    n_lat = bs * ts
    tiles = dict(n_ctx_tiles=n_ctx // ROW_TILE, tiles_per_latent=ts // ROW_TILE)

    cond = jnp.concatenate([c_ctx[None, :], c, jnp.zeros((SUBLANES - 1 - bs, d), F32)], axis=0)
    mod_all = modulation(cond, w_mod, b_mod).reshape(depth, SUBLANES, N_MOD, d)
    mod_all = jnp.pad(mod_all, ((0, 0), (0, 0), (0, SUBLANES - N_MOD), (0, 0)))

    w_in_p = _pad_cols(w_in).astype(BF16)
    w_out_b = w_out.astype(BF16)
    router_w_p = jnp.pad(router_w, ((0, 0), (0, 0), (0, LANES - N_EXPERTS)))
    router_b_p = jnp.pad(router_b, ((0, 0), (0, LANES - N_EXPERTS)))[:, None, :]

    pos = grid_pos_embed(ts // GRID_W, d)
    x = jnp.concatenate([x_prompt.reshape(n_ctx, d), x_sample.reshape(n_lat, d)], axis=0)
    delta = jnp.concatenate([jnp.zeros((n_ctx, d), F32), jnp.tile(pos, (bs, 1))], axis=0)

    z_gla = jnp.zeros((bp, 2, GLA_H, GLA_DK, GLA_DV), F32)
    st_dn, st_rw, st_gla = [], [], []
    for l in range(depth):
        mod = mod_all[l]
        x, u_dn, u_rw, u_gla = inproj(x, delta, mod_all[max(l - 1, 0)], mod, norm_mix[l][None, :], w_in_p[l],
                                      gate_row=None if l == 0 else 5, **tiles)
        o_dn, o_rw, o_gla = [], [], []
        for (lo, b, t, s_dn, s_rw, s_gla) in (
                (0, bp, tp, None, None, z_gla),
                (n_ctx, bs, ts, state_delta[:, l], state_rwkv[:, l], state_gla[:, l])):
            od, f_dn = deltanet_mixer(u_dn, lo, b, t, dn_conv[l], dn_a_log[l], dn_dt_bias[l], dn_norm[l], s_dn)
            orw, f_rw = rwkv7_mixer(u_rw, lo, b, t, rw_mu[l], rw_w0[l], rw_w2[l], rw_a0[l], rw_a2[l], rw_g2[l],
                                    rw_k_k[l], rw_k_a[l], rw_r_k[l], rw_ln_w[l], rw_ln_b[l], s_rw)
            og, f_gla = gla_mixer(u_gla[lo:lo + b * t].reshape(b, t, GLA_PAD), gla_gk2[l], gla_gk_b[l],
                                  gla_norm[l], s_gla)
            o_dn.append(od)
            o_rw.append(orw)
            o_gla.append(og)
            if lo == 0:
                st_dn.append(f_dn)
                st_rw.append(f_rw)
                st_gla.append(f_gla)
        x, h2, logits = outproj(jnp.concatenate(o_dn, axis=0), jnp.concatenate(o_rw, axis=0),
                                jnp.concatenate(o_gla, axis=0), x, mod, norm_ffn[l][None, :], w_out_b[l],
                                router_w_p[l], router_b_p[l], **tiles)
        delta = moe_ffn(l, h2, logits, moe_w_gu, moe_b_gu, moe_w_dn, moe_b_dn)
    y = final_norm(x, delta, mod_all[depth - 1], norm_out[None, :], **tiles)
    y_prompt = y[:n_ctx].reshape(bp, tp, d)
    y_sample = y[n_ctx:].reshape(bs, ts, d)
    return (y_prompt, y_sample, jnp.stack(st_dn, axis=1), jnp.stack(st_rw, axis=1), jnp.stack(st_gla, axis=1))
```

```python
import functools

import jax
import jax.numpy as jnp
from jax import lax
from jax.experimental import pallas as pl
from jax.experimental.pallas import tpu as pltpu

F32 = jnp.float32
BF16 = jnp.bfloat16
HIGHEST = lax.Precision.HIGHEST

D_MODEL = 1024
GRID_W = 64
POS_THETA = 10000.0
N_MOD = 6
RMS_EPS = 1e-6

DN_H, DN_DK, DN_DV = 4, 128, 128
DN_W = DN_H * DN_DV
CONV_K = 5

RW_H, RW_DH = 4, 64
RW_W = RW_H * RW_DH
RW_W_LORA, RW_A_LORA, RW_G_LORA = 64, 64, 128
RW_GN_EPS = 64e-5

GLA_H, GLA_DK, GLA_DV = 4, 32, 64
GLA_KW = GLA_H * GLA_DK
GLA_VW = GLA_H * GLA_DV
GLA_GK_LORA = 16
GLA_GATE_NORM = 16.0
GLA_CHUNK = 16

MIX_W = DN_W + RW_W + GLA_VW
DN_COLS = 4 * DN_W + 4 * DN_H
RW_COLS = 3 * RW_W + 2 * RW_W_LORA + 2 * RW_A_LORA + RW_G_LORA
GLA_COLS = 2 * GLA_KW + 2 * GLA_VW + 2 * GLA_GK_LORA

N_EXPERTS = 32
TOP_K = 4
D_FF = 1024
SWIGLU_LIMIT = 7.0
SWIGLU_ALPHA = 1.702

LANES = 128
SUBLANES = 8
VMEM_LIMIT = 56 * 1024 * 1024

DN_PAD = 17 * LANES
RW_PAD = RW_COLS
GLA_PAD = 7 * LANES
U_COLS = DN_PAD + RW_PAD + GLA_PAD

ROW_TILE = 256
MOE_BM = 256
CHUNK = 64
PACK = DN_H * CHUNK
GLA_CHUNKS_PER_STEP = 4
CHUNKS_PER_STEP = 2
N_MERGE = CHUNK.bit_length() - 1
N_MASKS = 4 + N_MERGE
HALO = SUBLANES
NEG_BIG = -1e30


def _cparams(sem):
    return pltpu.CompilerParams(dimension_semantics=sem, vmem_limit_bytes=VMEM_LIMIT)


def _silu(x):
    return x * jax.nn.sigmoid(x)


def _softplus(x):
    return jnp.maximum(x, 0.0) + jnp.log(1.0 + jnp.exp(-jnp.abs(x)))


def _bdot(a, b):
    return jnp.dot(a.astype(BF16), b.astype(BF16), preferred_element_type=F32)


def _bdot_nt(a, b):
    return lax.dot_general(a.astype(BF16), b.astype(BF16), (((1,), (1,)), ((), ())),
                           preferred_element_type=F32)


def _bdot_tn(a, b):
    return lax.dot_general(a.astype(BF16), b.astype(BF16), (((0,), (0,)), ((), ())),
                           preferred_element_type=F32)


def _hdot(a, b):
    return jnp.dot(a, b, precision=HIGHEST, preferred_element_type=F32)


def _mod_kernel(c_ref, w_ref, b_ref, o_ref):
    o_ref[0] = _hdot(_silu(c_ref[...]), w_ref[0]) + b_ref[0]


def modulation(cond, w_mod, b_mod):
    depth = w_mod.shape[0]
    n_out = w_mod.shape[2]
    tn = 1536
    return pl.pallas_call(
        _mod_kernel,
        grid=(depth, n_out // tn),
        in_specs=[
            pl.BlockSpec((SUBLANES, D_MODEL), lambda l, j: (0, 0)),
            pl.BlockSpec((1, D_MODEL, tn), lambda l, j: (l, 0, j)),
            pl.BlockSpec((1, 1, tn), lambda l, j: (l, 0, j)),
        ],
        out_specs=pl.BlockSpec((1, SUBLANES, tn), lambda l, j: (l, 0, j)),
        out_shape=jax.ShapeDtypeStruct((depth, SUBLANES, n_out), F32),
        compiler_params=_cparams(("arbitrary", "arbitrary")),
        name="modulation",
    )(cond, w_mod, b_mod.reshape(depth, 1, n_out))


def _tile_cond(i, n_ctx_tiles, tiles_per_latent):
    return jnp.where(i < n_ctx_tiles, 0, 1 + (i - n_ctx_tiles) // tiles_per_latent)


def _inproj_kernel(gate_row, x_ref, d_ref, mg_ref, mod_ref, g_ref, w_ref, xo_ref, udn_ref, urw_ref, ugla_ref):
    x = x_ref[...]
    if gate_row is None:
        x = x + d_ref[...]
    else:
        x = x + mg_ref[0, gate_row:gate_row + 1, :] * d_ref[...]
    xo_ref[...] = x
    y = x * lax.rsqrt(jnp.mean(x * x, axis=-1, keepdims=True) + RMS_EPS) * g_ref[...]
    h = (y * (1.0 + mod_ref[0, 1:2, :]) + mod_ref[0, 0:1, :]).astype(BF16)
    udn_ref[...] = jnp.dot(h, w_ref[:, :DN_PAD], preferred_element_type=F32)
    urw_ref[...] = jnp.dot(h, w_ref[:, DN_PAD:DN_PAD + RW_PAD], preferred_element_type=F32)
    ugla_ref[...] = jnp.dot(h, w_ref[:, DN_PAD + RW_PAD:], preferred_element_type=F32)


def inproj(x, delta, mod_gate, mod, g, w, n_ctx_tiles, tiles_per_latent, gate_row):
    n = x.shape[0]
    cond_of = functools.partial(_tile_cond, n_ctx_tiles=n_ctx_tiles, tiles_per_latent=tiles_per_latent)
    rows = lambda width: pl.BlockSpec((ROW_TILE, width), lambda i: (i, 0))
    modspec = pl.BlockSpec((1, SUBLANES, D_MODEL), lambda i: (cond_of(i), 0, 0))
    return pl.pallas_call(
        functools.partial(_inproj_kernel, gate_row),
        grid=(n // ROW_TILE,),
        in_specs=[rows(D_MODEL), rows(D_MODEL), modspec, modspec,
                  pl.BlockSpec((1, D_MODEL), lambda i: (0, 0)),
                  pl.BlockSpec((D_MODEL, U_COLS), lambda i: (0, 0))],
        out_specs=[rows(D_MODEL), rows(DN_PAD), rows(RW_PAD), rows(GLA_PAD)],
        out_shape=[jax.ShapeDtypeStruct((n, D_MODEL), F32),
                   jax.ShapeDtypeStruct((n, DN_PAD), F32),
                   jax.ShapeDtypeStruct((n, RW_PAD), F32),
                   jax.ShapeDtypeStruct((n, GLA_PAD), F32)],
        compiler_params=_cparams(("parallel",)),
        name="inproj",
    )(x, delta, mod_gate, mod, g, w)


def _outproj_kernel(odn_ref, orw_ref, ogla_ref, x_ref, mod_ref, g_ref, w_ref, rw_ref, rb_ref,
                    x2_ref, h2_ref, lg_ref):
    mix = (jnp.dot(odn_ref[...].astype(BF16), w_ref[:DN_W, :], preferred_element_type=F32)
           + jnp.dot(orw_ref[...].astype(BF16), w_ref[DN_W:DN_W + RW_W, :], preferred_element_type=F32)
           + jnp.dot(ogla_ref[...].astype(BF16), w_ref[DN_W + RW_W:, :], preferred_element_type=F32))
    x2 = x_ref[...] + mod_ref[0, 2:3, :] * mix
    x2_ref[...] = x2
    y = x2 * lax.rsqrt(jnp.mean(x2 * x2, axis=-1, keepdims=True) + RMS_EPS) * g_ref[...]
    h2 = y * (1.0 + mod_ref[0, 4:5, :]) + mod_ref[0, 3:4, :]
    h2_ref[...] = h2.astype(BF16)
    lg_ref[...] = _hdot(h2, rw_ref[...]) + rb_ref[...]


def outproj(o_dn, o_rw, o_gla, x, mod, g, w, router_w, router_b, n_ctx_tiles, tiles_per_latent):
    n = x.shape[0]
    cond_of = functools.partial(_tile_cond, n_ctx_tiles=n_ctx_tiles, tiles_per_latent=tiles_per_latent)
    rows = lambda width: pl.BlockSpec((ROW_TILE, width), lambda i: (i, 0))
    return pl.pallas_call(
        _outproj_kernel,
        grid=(n // ROW_TILE,),
        in_specs=[rows(DN_W), rows(RW_W), rows(GLA_VW), rows(D_MODEL),
                  pl.BlockSpec((1, SUBLANES, D_MODEL), lambda i: (cond_of(i), 0, 0)),
                  pl.BlockSpec((1, D_MODEL), lambda i: (0, 0)),
                  pl.BlockSpec((MIX_W, D_MODEL), lambda i: (0, 0)),
                  pl.BlockSpec((D_MODEL, LANES), lambda i: (0, 0)),
                  pl.BlockSpec((1, LANES), lambda i: (0, 0))],
        out_specs=[rows(D_MODEL), rows(D_MODEL), rows(LANES)],
        out_shape=[jax.ShapeDtypeStruct((n, D_MODEL), F32),
                   jax.ShapeDtypeStruct((n, D_MODEL), BF16),
                   jax.ShapeDtypeStruct((n, LANES), F32)],
        compiler_params=_cparams(("parallel",)),
        name="outproj",
    )(o_dn, o_rw, o_gla, x, mod, g, w, router_w, router_b)


def _final_kernel(x_ref, d_ref, mod_ref, g_ref, y_ref):
    x = x_ref[...] + mod_ref[0, 5:6, :] * d_ref[...]
    y_ref[...] = x * lax.rsqrt(jnp.mean(x * x, axis=-1, keepdims=True) + RMS_EPS) * g_ref[...]


def final_norm(x, delta, mod, g, n_ctx_tiles, tiles_per_latent):
    n = x.shape[0]
    cond_of = functools.partial(_tile_cond, n_ctx_tiles=n_ctx_tiles, tiles_per_latent=tiles_per_latent)
    rows = pl.BlockSpec((ROW_TILE, D_MODEL), lambda i: (i, 0))
    return pl.pallas_call(
        _final_kernel,
        grid=(n // ROW_TILE,),
        in_specs=[rows, rows,
                  pl.BlockSpec((1, SUBLANES, D_MODEL), lambda i: (cond_of(i), 0, 0)),
                  pl.BlockSpec((1, D_MODEL), lambda i: (0, 0))],
        out_specs=rows,
        out_shape=jax.ShapeDtypeStruct((n, D_MODEL), F32),
        compiler_params=_cparams(("parallel",)),
        name="final_norm",
    )(x, delta, mod, g)


def _chunk_masks(mask_scr):
    r = lax.broadcasted_iota(jnp.int32, (PACK, PACK), 0)
    c = lax.broadcasted_iota(jnp.int32, (PACK, PACK), 1)
    same = (r // CHUNK) == (c // CHUNK)
    tr = r % CHUNK
    tc = c % CHUNK
    mask_scr[0] = jnp.where(same & (tc <= tr), 0.0, NEG_BIG)
    mask_scr[1] = jnp.where(same & (tc >= tr), 0.0, NEG_BIG)
    mask_scr[2] = (same & (tc < tr)).astype(F32)
    mask_scr[3] = (same & (tc > tr)).astype(F32)
    for i in range(N_MERGE):
        m = 1 << i
        mask_scr[4 + i] = (((r // (2 * m)) == (c // (2 * m))) & ((r // m) != (c // m))).astype(F32)


def _cum_matrix(d):
    i = lax.broadcasted_iota(jnp.int32, (CHUNK, CHUNK), 0)
    j = lax.broadcasted_iota(jnp.int32, (CHUNK, CHUNK), 1)
    return ((j <= i) if d == 0 else (j >= i)).astype(F32)


def _tri_inverse(a, mask_scr):
    n = a * mask_scr[4]
    for i in range(1, N_MERGE):
        c = a * mask_scr[4 + i]
        x = c + _bdot(c, n)
        yield
        n = n + x + _bdot(n, x)
        yield
    return n


def _interleave(gens):
    results = [None] * len(gens)
    active = list(range(len(gens)))
    while active:
        for i in list(active):
            try:
                next(gens[i])
            except StopIteration as stop:
                results[i] = stop.value
                active.remove(i)
    return results


def _fill_halo(pad_scr, src_ref, t, width):
    zeros = jnp.zeros((HALO, width), F32)
    pad_scr[0:HALO, :] = zeros
    pad_scr[HALO + t:2 * HALO + t, :] = zeros

    def copy(i, carry):
        r0 = pl.multiple_of(i * CHUNK, CHUNK)
        pad_scr[pl.ds(HALO + r0, CHUNK), :] = src_ref[pl.ds(r0, CHUNK), :width]
        return carry

    lax.fori_loop(0, t // CHUNK, copy, 0)


def _dn_kernel(t, has_s0, u_ref, cw_ref, gp_ref, nw_ref, *rest):
    if has_s0:
        s0_ref, o_ref, sfin_ref = rest[:3]
        scr = rest[3:]
    else:
        o_ref, sfin_ref = rest[:2]
        scr = rest[2:]
    pad_scr, q_scr, k_scr, v_scr, bg_scr, acc_scr, s_scr, mask_scr = scr
    n_chunks = t // CHUNK
    qkv_w = 3 * DN_W

    if has_s0:
        s_scr[...] = s0_ref[0]
    else:
        s_scr[...] = jnp.zeros_like(s_scr)
    acc_scr[...] = jnp.zeros_like(acc_scr)
    _chunk_masks(mask_scr)
    _fill_halo(pad_scr, u_ref, t, qkv_w)

    lane = lax.broadcasted_iota(jnp.int32, (1, LANES), 1)
    neg_a = -jnp.exp(gp_ref[0:1, :])
    dt_b = gp_ref[1:2, :]

    def prep(i, carry):
        r0 = pl.multiple_of(i * CHUNK, CHUNK)
        rows = pl.ds(r0, CHUNK)
        for cg in range(qkv_w // LANES):
            cols = slice(cg * LANES, (cg + 1) * LANES)
            win = pad_scr[pl.ds(r0, CHUNK + 2 * HALO), cols]
            acc = None
            for j in range(CONV_K):
                off = HALO - CONV_K // 2 + j
                term = cw_ref[j:j + 1, cols] * win[off:off + CHUNK, :]
                acc = term if acc is None else acc + term
            y = _silu(acc)
            which, hc = divmod(cg, DN_H)
            hcols = slice(hc * LANES, (hc + 1) * LANES)
            if which < 2:
                y = y * lax.rsqrt(jnp.sum(y * y, axis=-1, keepdims=True) + 1e-6)
            if which == 0:
                q_scr[rows, hcols] = y * DN_DK ** -0.5
            elif which == 1:
                k_scr[rows, hcols] = y
            else:
                v_scr[rows, hcols] = y
        ub = u_ref[rows, 4 * DN_W:4 * DN_W + LANES]
        bg_scr[rows, :] = jnp.where(lane < 2 * DN_H, jax.nn.sigmoid(ub), neg_a * _softplus(ub + dt_b))
        return carry

    lax.fori_loop(0, n_chunks, prep, 0)

    def stack(ref, rows):
        return jnp.concatenate([ref[rows, h * LANES:(h + 1) * LANES] for h in range(DN_H)], axis=0)

    def local(d, rows):
        last = CHUNK - 1 if d == 0 else 0
        bg = bg_scr[rows, :]
        g_cum = _hdot(_cum_matrix(d), bg)
        lanes_g = [2 * DN_H + d * DN_H + h for h in range(DN_H)]
        beta = jnp.concatenate([bg[:, d * DN_H + h:d * DN_H + h + 1] for h in range(DN_H)], axis=0)
        g_col = jnp.concatenate([g_cum[:, l:l + 1] for l in lanes_g], axis=0)
        g_tot = jnp.concatenate(
            [jnp.broadcast_to(g_cum[last:last + 1, l:l + 1], (CHUNK, 1)) for l in lanes_g], axis=0)
        g_row = jnp.transpose(jnp.broadcast_to(g_col, (PACK, LANES)))[0:1, :]
        decay = jnp.exp(g_col - g_row + mask_scr[d])
        qc = stack(q_scr, rows)
        kc = stack(k_scr, rows)
        vc = stack(v_scr, rows)
        qkk = _bdot_nt(jnp.concatenate([qc, kc], axis=0), kc)
        yield
        qk = (qkk[:PACK] * decay).astype(BF16)
        low = (beta * qkk[PACK:]) * (decay * mask_scr[2 + d])
        nm = yield from _tri_inverse(-low, mask_scr)
        eg = jnp.exp(g_col)
        rhs = jnp.concatenate([beta * vc, (beta * eg) * kc], axis=1)
        sol = rhs + _bdot(nm, rhs)
        yield
        wq = [jnp.concatenate([sol[h * CHUNK:(h + 1) * CHUNK, DN_DV:],
                               (qc * eg)[h * CHUNK:(h + 1) * CHUNK]], axis=0).astype(BF16)
              for h in range(DN_H)]
        k_g = (kc * jnp.exp(g_tot - g_col)).astype(BF16)
        return sol[:, :DN_DV], wq, qk, k_g, jnp.exp(g_tot)

    def carried(d, chunk_rows, chunk_parts):
        for rows, parts in zip(chunk_rows, chunk_parts):
            u_t, wq, qk, k_g, e_tot = parts
            us = []
            oq = []
            for h in range(DN_H):
                hr = slice(h * CHUNK, (h + 1) * CHUNK)
                ws = _bdot(wq[h], s_scr[d * DN_H + h])
                us.append(u_t[hr] - ws[:CHUNK])
                oq.append(ws[CHUNK:])
            yield
            u_all = jnp.concatenate(us, axis=0).astype(BF16)
            o_all = jnp.concatenate(oq, axis=0) + _bdot(qk, u_all)
            for h in range(DN_H):
                hr = slice(h * CHUNK, (h + 1) * CHUNK)
                idx = d * DN_H + h
                s_scr[idx] = e_tot[h * CHUNK:h * CHUNK + 1, :] * s_scr[idx] + _bdot_tn(k_g[hr], u_all[hr])
                acc_scr[rows, h * LANES:(h + 1) * LANES] += o_all[hr]
            yield

    def body(n, carry):
        rows = [[], []]
        for j in range(CHUNKS_PER_STEP):
            for d in range(2):
                ch = n * CHUNKS_PER_STEP + j
                ch = ch if d == 0 else n_chunks - 1 - ch
                rows[d].append(pl.ds(pl.multiple_of(ch * CHUNK, CHUNK), CHUNK))
        flat = [(d, r) for d in range(2) for r in rows[d]]
        parts = _interleave([local(d, r) for d, r in flat])
        by_dir = [[p for (dd, _), p in zip(flat, parts) if dd == d] for d in range(2)]
        _interleave([carried(d, rows[d], by_dir[d]) for d in range(2)])
        return carry

    lax.fori_loop(0, n_chunks // CHUNKS_PER_STEP, body, 0)
    sfin_ref[0] = s_scr[...]

    def post(i, carry):
        rows = pl.ds(pl.multiple_of(i * CHUNK, CHUNK), CHUNK)
        for h in range(DN_H):
            cols = slice(h * LANES, (h + 1) * LANES)
            o = acc_scr[rows, cols]
            on = o * lax.rsqrt(jnp.mean(o * o, axis=-1, keepdims=True) + RMS_EPS) * nw_ref[...]
            z = u_ref[rows, 3 * DN_W + h * LANES:3 * DN_W + (h + 1) * LANES]
            o_ref[rows, cols] = on * _silu(z)
        return carry

    lax.fori_loop(0, n_chunks, post, 0)


def deltanet_mixer(u_dn, row0, b, t, conv_w, a_log, dt_bias, norm_w, s0):
    nh2 = 2 * DN_H
    blk0 = row0 // t
    cw = jnp.pad(conv_w, ((0, SUBLANES - CONV_K), (0, 0)))
    gp = jnp.zeros((SUBLANES, LANES), F32)
    gp = gp.at[0, nh2:2 * nh2].set(a_log.reshape(-1)).at[1, nh2:2 * nh2].set(dt_bias.reshape(-1))
    st = pl.BlockSpec((1, nh2, DN_DK, DN_DV), lambda i: (i, 0, 0, 0))
    in_specs = [pl.BlockSpec((t, DN_PAD), lambda i: (blk0 + i, 0)),
                pl.BlockSpec((SUBLANES, 3 * DN_W), lambda i: (0, 0)),
                pl.BlockSpec((SUBLANES, LANES), lambda i: (0, 0)),
                pl.BlockSpec((1, DN_DV), lambda i: (0, 0))]
    args = [u_dn, cw, gp, norm_w[None, :]]
    if s0 is not None:
        in_specs.append(st)
        args.append(s0.reshape(b, nh2, DN_DK, DN_DV))
    o, sfin = pl.pallas_call(
        functools.partial(_dn_kernel, t, s0 is not None),
        grid=(b,),
        in_specs=in_specs,
        out_specs=[pl.BlockSpec((t, DN_W), lambda i: (i, 0)), st],
        out_shape=[jax.ShapeDtypeStruct((b * t, DN_W), F32),
                   jax.ShapeDtypeStruct((b, nh2, DN_DK, DN_DV), F32)],
        scratch_shapes=[pltpu.VMEM((t + 2 * HALO, 3 * DN_W), F32),
                        pltpu.VMEM((t, DN_W), F32), pltpu.VMEM((t, DN_W), F32), pltpu.VMEM((t, DN_W), F32),
                        pltpu.VMEM((t, LANES), F32), pltpu.VMEM((t, DN_W), F32),
                        pltpu.VMEM((nh2, DN_DK, DN_DV), F32),
                        pltpu.VMEM((N_MASKS, PACK, PACK), F32)],
        compiler_params=_cparams(("parallel",)),
        name="deltanet",
    )(*args)
    return o, sfin.reshape(b, 2, DN_H, DN_DK, DN_DV)


def _rw_kernel(t, has_s0, u_ref, mu_ref, w2_ref, a2_ref, g2_ref, wa0_ref, pv_ref, *rest):
    if has_s0:
        s0_ref, o_ref, sfin_ref = rest[:3]
        scr = rest[3:]
    else:
        o_ref, sfin_ref = rest[:2]
        scr = rest[2:]
    pad_scr, r_scr, v_scr, kk_scr, gate_scr, bonus_scr, lw_scr, kd_scr, b_scr, y_scr, z_scr, mask_scr = scr
    n_chunks = t // CHUNK
    w = RW_W

    if has_s0:
        z_scr[...] = s0_ref[0]
    else:
        z_scr[...] = jnp.zeros_like(z_scr)
    y_scr[...] = jnp.zeros_like(y_scr)
    _chunk_masks(mask_scr)
    _fill_halo(pad_scr, u_ref, t, RW_COLS)

    low_half = lax.broadcasted_iota(jnp.int32, (1, LANES), 1) < RW_DH

    def seg_sum(x):
        outs = []
        for gi in range(w // LANES):
            xg = x[:, gi * LANES:(gi + 1) * LANES]
            lo = jnp.sum(jnp.where(low_half, xg, 0.0), axis=-1, keepdims=True)
            hi = jnp.sum(jnp.where(low_half, 0.0, xg), axis=-1, keepdims=True)
            outs.append(jnp.where(low_half, lo, hi))
        return jnp.concatenate(outs, axis=1)

    head_mask = lax.broadcasted_iota(jnp.int32, (PACK, w), 0) // CHUNK == \
        lax.broadcasted_iota(jnp.int32, (PACK, w), 1) // RW_DH

    k_k = pv_ref[0:1, :]
    k_a = pv_ref[1:2, :]
    r_k = pv_ref[2:3, :]
    ln_w = pv_ref[3:4, :]
    ln_b = pv_ref[4:5, :]

    def prep(i, carry):
        r0 = pl.multiple_of(i * CHUNK, CHUNK)
        rows = pl.ds(r0, CHUNK)
        xs = []
        for cg in range(RW_COLS // LANES):
            cols = slice(cg * LANES, (cg + 1) * LANES)
            win = pad_scr[pl.ds(r0, CHUNK + 2 * HALO), cols]
            cur = win[HALO:HALO + CHUNK, :]
            nb = 0.5 * (win[HALO - 1:HALO - 1 + CHUNK, :] + win[HALO + 1:HALO + 1 + CHUNK, :])
            xs.append(cur + mu_ref[:, cols] * (nb - cur))
        x = jnp.concatenate(xs, axis=1)
        r = x[:, :w]
        k = x[:, w:2 * w]
        v = x[:, 2 * w:3 * w]
        wd = x[:, 3 * w:3 * w + 2 * RW_W_LORA]
        ad = x[:, 3 * w + 2 * RW_W_LORA:3 * w + 2 * RW_W_LORA + 2 * RW_A_LORA]
        gd = x[:, 3 * w + 2 * RW_W_LORA + 2 * RW_A_LORA:]
        w_log = -_softplus(-(wa0_ref[0:1, :] + _bdot(jnp.tanh(wd), w2_ref[...]))) - 0.5
        al = jax.nn.sigmoid(wa0_ref[1:2, :] + _bdot(ad, a2_ref[...]))
        kk = k * k_k
        kk = kk * lax.rsqrt(seg_sum(kk * kk) + 1e-6)
        r_scr[rows, :] = r
        v_scr[rows, :] = v
        kk_scr[rows, :] = kk
        gate_scr[rows, :] = _bdot(jax.nn.sigmoid(gd), g2_ref[...])
        bonus_scr[rows, :] = seg_sum(r * k * r_k) * v
        lw_scr[rows, :] = -jnp.exp(w_log)
        for d in range(2):
            dc = slice(d * w, (d + 1) * w)
            kd_scr[rows, dc] = k * (1.0 + (al[:, dc] - 1.0) * k_a)
            b_scr[rows, dc] = kk * al[:, dc]
        return carry

    lax.fori_loop(0, n_chunks, prep, 0)

    def stack(x):
        return jnp.where(head_mask, jnp.concatenate([x] * RW_H, axis=0), 0.0)

    def local(d, rows):
        dc = slice(d * w, (d + 1) * w)
        last = CHUNK - 1 if d == 0 else 0
        lw = lw_scr[rows, dc]
        lc = _hdot(_cum_matrix(d), lw)
        l_tot = lc[last:last + 1, :]
        e_in = jnp.exp(lc)
        e_ex = jnp.exp(lc - lw)
        e_neg = jnp.exp(-lc)
        e_rem = jnp.exp(l_tot - lc)
        kk = kk_scr[rows, :]
        kd = kd_scr[rows, dc]
        bb = b_scr[rows, dc]
        a_t = stack(-kk * e_ex)
        r_t = stack(r_scr[rows, :] * e_in).astype(BF16)
        b_t = stack(bb * e_neg).astype(BF16)
        k_t = stack(kd * e_neg).astype(BF16)
        kb_h = jnp.concatenate([stack(kd * e_rem), stack(bb * e_rem)], axis=0).astype(BF16)
        v_m = stack(v_scr[rows, :]).astype(BF16)
        g = _bdot_nt(jnp.concatenate([a_t.astype(BF16), r_t], axis=0),
                     jnp.concatenate([b_t, k_t], axis=0))
        yield
        strict = mask_scr[2 + d]
        incl = jnp.where(mask_scr[d] == 0.0, 1.0, 0.0)
        a_ab = g[:PACK, :PACK] * strict
        a_k = jnp.concatenate([g[:PACK, PACK:] * strict, g[PACK:, PACK:] * incl], axis=0).astype(BF16)
        a_rb = (g[PACK:, :PACK] * incl).astype(BF16)
        nm = yield from _tri_inverse(a_ab, mask_scr)
        av = _bdot(a_k, v_m)
        yield
        rhs = jnp.concatenate([av[:PACK], a_t], axis=1)
        sol = rhs + _bdot(nm, rhs)
        yield
        ar = jnp.concatenate([sol[:, w:].astype(BF16), r_t], axis=0)
        return sol[:, :w], ar, av[PACK:], a_rb, v_m, kb_h, jnp.exp(l_tot)

    def carried(d, chunk_rows, chunk_parts):
        for rows, parts in zip(chunk_rows, chunk_parts):
            p0, ar, y0, a_rb, v_m, kb_h, e_tot = parts
            zt = z_scr[d]
            az = _bdot_nt(ar, zt)
            yield
            p = (p0 + az[:PACK]).astype(BF16)
            y = y0 + az[PACK:] + _bdot(a_rb, p)
            y_scr[rows, :] += (y[0:CHUNK] + y[CHUNK:2 * CHUNK]) + (y[2 * CHUNK:3 * CHUNK] + y[3 * CHUNK:])
            yield
            z_scr[d] = zt * e_tot + _bdot_tn(jnp.concatenate([v_m, p], axis=0), kb_h)
            yield

    def body(n, carry):
        rows = [[], []]
        for j in range(CHUNKS_PER_STEP):
            for d in range(2):
                ch = n * CHUNKS_PER_STEP + j
                ch = ch if d == 0 else n_chunks - 1 - ch
                rows[d].append(pl.ds(pl.multiple_of(ch * CHUNK, CHUNK), CHUNK))
        flat = [(d, r) for d in range(2) for r in rows[d]]
        parts = _interleave([local(d, r) for d, r in flat])
        by_dir = [[p for (dd, _), p in zip(flat, parts) if dd == d] for d in range(2)]
        _interleave([carried(d, rows[d], by_dir[d]) for d in range(2)])
        return carry

    lax.fori_loop(0, n_chunks // CHUNKS_PER_STEP, body, 0)
    sfin_ref[0] = z_scr[...]

    def post(i, carry):
        rows = pl.ds(pl.multiple_of(i * CHUNK, CHUNK), CHUNK)
        y = y_scr[rows, :]
        m = seg_sum(y) * (1.0 / RW_DH)
        yc = y - m
        var = seg_sum(yc * yc) * (1.0 / RW_DH)
        yn = yc * lax.rsqrt(var + RW_GN_EPS) * ln_w + ln_b
        o_ref[rows, :] = (yn + bonus_scr[rows, :]) * gate_scr[rows, :]
        return carry

    lax.fori_loop(0, n_chunks, post, 0)


def _dir_blocks(m):
    z = jnp.zeros_like(m[0])
    return jnp.concatenate([jnp.concatenate([m[0], z], axis=1), jnp.concatenate([z, m[1]], axis=1)], axis=0)


def rwkv7_mixer(u_rw, row0, b, t, mu, w0, w2, a0, a2, g2, k_k, k_a, r_k, ln_w, ln_b, s0):
    w = RW_W
    blk0 = row0 // t
    w2b = _dir_blocks(w2).astype(BF16)
    a2b = _dir_blocks(a2).astype(BF16)
    wa0 = jnp.zeros((SUBLANES, 2 * w), F32).at[0].set(w0.reshape(-1)).at[1].set(a0.reshape(-1))
    pv = jnp.zeros((SUBLANES, w), F32)
    for i, p in enumerate((k_k, k_a, r_k.reshape(-1), ln_w, ln_b)):
        pv = pv.at[i].set(p)
    full = lambda a: pl.BlockSpec(a.shape, lambda i: (0,) * a.ndim)
    st = pl.BlockSpec((1, 2, w, w), lambda i: (i, 0, 0, 0))
    args = [u_rw, mu[None, :], w2b, a2b, g2.astype(BF16), wa0, pv]
    in_specs = [pl.BlockSpec((t, RW_COLS), lambda i: (blk0 + i, 0))] + [full(a) for a in args[1:]]
    eye_h = jnp.eye(RW_H, dtype=F32)
    if s0 is not None:
        in_specs.append(st)
        args.append(jnp.einsum('bdhvk,hg->bdhvgk', s0, eye_h).reshape(b, 2, w, w))
    seq = lambda width: pltpu.VMEM((t, width), F32)
    o, sfin = pl.pallas_call(
        functools.partial(_rw_kernel, t, s0 is not None),
        grid=(b,),
        in_specs=in_specs,
        out_specs=[pl.BlockSpec((t, w), lambda i: (i, 0)), st],
        out_shape=[jax.ShapeDtypeStruct((b * t, w), F32), jax.ShapeDtypeStruct((b, 2, w, w), F32)],
        scratch_shapes=[pltpu.VMEM((t + 2 * HALO, RW_COLS), F32),
                        seq(w), seq(w), seq(w), seq(w), seq(w), seq(2 * w), seq(2 * w), seq(2 * w), seq(w),
                        pltpu.VMEM((2, w, w), F32), pltpu.VMEM((N_MASKS, PACK, PACK), F32)],
        compiler_params=_cparams(("parallel",)),
        name="rwkv7",
    )(*args)
    sf = sfin.reshape(b, 2, RW_H, RW_DH, RW_H, RW_DH)
    return o, jnp.einsum('bdhvhk->bdhvk', sf)


def _gla_kernel(t, has_s0, u_ref, gk2_ref, gkb_ref, nw_ref, *rest):
    if has_s0:
        s0_ref, o_ref, sfin_ref, s_scr, gk_scr, acc_scr = rest
        s_scr[...] = s0_ref[0]
    else:
        o_ref, sfin_ref, s_scr, gk_scr, acc_scr = rest
        s_scr[...] = jnp.zeros_like(s_scr)
    c = GLA_CHUNK
    n_chunks = t // c
    acc_scr[...] = jnp.zeros_like(acc_scr)
    q_cols = slice(0, GLA_KW)
    k_cols = slice(GLA_KW, 2 * GLA_KW)
    v_cols = slice(2 * GLA_KW, 2 * GLA_KW + GLA_VW)
    g_off = 2 * GLA_KW + GLA_VW
    lora_cols = slice(g_off + GLA_VW, g_off + GLA_VW + LANES)

    def prep(i, carry):
        rows = pl.ds(pl.multiple_of(i * CHUNK, CHUNK), CHUNK)
        z = _bdot(u_ref[rows, lora_cols], gk2_ref[...]) + gkb_ref[...]
        gk_scr[rows, :] = -_softplus(-z) * (1.0 / GLA_GATE_NORM)
        return carry

    lax.fori_loop(0, t // CHUNK, prep, 0)

    ii = lax.broadcasted_iota(jnp.int32, (c, c), 0)
    jj = lax.broadcasted_iota(jnp.int32, (c, c), 1)
    cum_m = ((ii >= jj).astype(F32), (ii <= jj).astype(F32))
    row_i = lax.broadcasted_iota(jnp.int32, (c, 1), 0)
    hk = lax.broadcasted_iota(jnp.int32, (GLA_KW, GLA_VW), 0) // GLA_DK
    hv = lax.broadcasted_iota(jnp.int32, (GLA_KW, GLA_VW), 1) // GLA_DV
    head_sum = (hk == hv).astype(BF16)
    bd_mask = (lax.broadcasted_iota(jnp.int32, (GLA_VW, GLA_KW), 0) // GLA_DV
               == lax.broadcasted_iota(jnp.int32, (GLA_VW, GLA_KW), 1) // GLA_DK)

    def local(d, rows):
        qc = u_ref[rows, q_cols] * GLA_DK ** -0.5
        kc = u_ref[rows, k_cols]
        vc = u_ref[rows, v_cols]
        g = gk_scr[rows, d * GLA_KW:(d + 1) * GLA_KW]
        bc = _hdot(cum_m[d], g)
        yield
        last = c - 1 if d == 0 else 0
        bc_last = bc[last:last + 1, :]
        terms = []
        for j in range(c):
            keep = (row_i >= j) if d == 0 else (row_i <= j)
            dec = jnp.exp(jnp.where(keep, bc - bc[j:j + 1, :], NEG_BIG))
            terms.append(qc * kc[j:j + 1, :] * dec)
        t_all = jnp.concatenate(terms, axis=0).astype(BF16)
        att = jnp.dot(t_all, head_sum, preferred_element_type=F32)
        yield
        o = None
        for j in range(c):
            term = att[j * c:(j + 1) * c, :] * vc[j:j + 1, :]
            o = term if o is None else o + term
        q_g = (qc * jnp.exp(bc)).astype(BF16)
        upd = jnp.where(bd_mask, _bdot_tn(vc, kc * jnp.exp(bc_last - bc)), 0.0)
        yield
        return o, q_g, upd, jnp.exp(bc_last)

    def carried(d, chunk_rows, chunk_parts):
        for rows, parts in zip(chunk_rows, chunk_parts):
            o, q_g, upd, e_last = parts
            st = s_scr[d]
            acc_scr[rows, :] += o + _bdot_nt(q_g, st)
            s_scr[d] = st * e_last + upd
            yield

    def body(n, carry):
        rows = [[], []]
        for j in range(GLA_CHUNKS_PER_STEP):
            for d in range(2):
                ch = n * GLA_CHUNKS_PER_STEP + j
                ch = ch if d == 0 else n_chunks - 1 - ch
                rows[d].append(pl.ds(pl.multiple_of(ch * c, c), c))
        flat = [(d, r) for d in range(2) for r in rows[d]]
        parts = _interleave([local(d, r) for d, r in flat])
        by_dir = [[p for (dd, _), p in zip(flat, parts) if dd == d] for d in range(2)]
        _interleave([carried(d, rows[d], by_dir[d]) for d in range(2)])
        return carry

    lax.fori_loop(0, n_chunks // GLA_CHUNKS_PER_STEP, body, 0)
    sfin_ref[0] = s_scr[...]

    low_half = lax.broadcasted_iota(jnp.int32, (1, LANES), 1) < GLA_DV

    def post(i, carry):
        rows = pl.ds(pl.multiple_of(i * CHUNK, CHUNK), CHUNK)
        for gi in range(GLA_VW // LANES):
            cols = slice(gi * LANES, (gi + 1) * LANES)
            o = acc_scr[rows, cols]
            sq = o * o
            lo = jnp.sum(jnp.where(low_half, sq, 0.0), axis=-1, keepdims=True)
            hi = jnp.sum(jnp.where(low_half, 0.0, sq), axis=-1, keepdims=True)
            ms = jnp.where(low_half, lo, hi) * (1.0 / GLA_DV)
            gate = u_ref[rows, g_off + gi * LANES:g_off + (gi + 1) * LANES]
            o_ref[rows, cols] = o * lax.rsqrt(ms + RMS_EPS) * nw_ref[:, cols] * _silu(gate)
        return carry

    lax.fori_loop(0, t // CHUNK, post, 0)


def gla_mixer(u_gla, row0, b, t, gk2, gk_b, norm_w, s0):
    blk0 = row0 // t
    gk2b = jnp.pad(_dir_blocks(gk2), ((0, LANES - 2 * GLA_GK_LORA), (0, 0))).astype(BF16)
    st = pl.BlockSpec((1, 2, GLA_VW, GLA_KW), lambda i: (i, 0, 0, 0))
    in_specs = [pl.BlockSpec((t, GLA_PAD), lambda i: (blk0 + i, 0)),
                pl.BlockSpec((LANES, 2 * GLA_KW), lambda i: (0, 0)),
                pl.BlockSpec((1, 2 * GLA_KW), lambda i: (0, 0)),
                pl.BlockSpec((1, GLA_VW), lambda i: (0, 0))]
    args = [u_gla, gk2b, gk_b.reshape(1, 2 * GLA_KW), jnp.tile(norm_w, GLA_H)[None, :]]
    if s0 is not None:
        eye_h = jnp.eye(GLA_H, dtype=F32)
        in_specs.append(st)
        args.append(jnp.einsum('bdhkv,hg->bdhvgk', s0, eye_h).reshape(b, 2, GLA_VW, GLA_KW))
    o, sfin_t = pl.pallas_call(
        functools.partial(_gla_kernel, t, s0 is not None),
        grid=(b,),
        in_specs=in_specs,
        out_specs=[pl.BlockSpec((t, GLA_VW), lambda i: (i, 0)), st],
        out_shape=[jax.ShapeDtypeStruct((b * t, GLA_VW), F32),
                   jax.ShapeDtypeStruct((b, 2, GLA_VW, GLA_KW), F32)],
        scratch_shapes=[pltpu.VMEM((2, GLA_VW, GLA_KW), F32), pltpu.VMEM((t, 2 * GLA_KW), F32),
                        pltpu.VMEM((t, GLA_VW), F32)],
        compiler_params=_cparams(("parallel",)),
        name="gla",
    )(*args)
    sf = sfin_t.reshape(b, 2, GLA_H, GLA_DV, GLA_H, GLA_DK)
    return o, jnp.einsum('bdhvhk->bdhkv', sf)


def _moe_kernel(layer, be_ref, nxt_ref, slot_ref, nb_ref, x_ref, wgu_hbm, bgu_ref, wdn_hbm, bdn_ref, y_ref,
                wgu_f32, wdn_f32, wgu_bf, wdn_bf, sem):
    i = pl.program_id(0)
    used = i < nb_ref[0]
    e = be_ref[i]
    prev = be_ref[jnp.maximum(i - 1, 0)]
    fresh = jnp.logical_or(i == 0, e != prev)
    slot = slot_ref[i]

    def weight_copies(expert, s):
        return (pltpu.make_async_copy(wgu_hbm.at[layer, expert], wgu_f32.at[s], sem.at[0, s]),
                pltpu.make_async_copy(wdn_hbm.at[layer, expert], wdn_f32.at[s], sem.at[1, s]))

    @pl.when(jnp.logical_and(used, i == 0))
    def _():
        for cp in weight_copies(e, slot):
            cp.start()

    @pl.when(jnp.logical_and(used, fresh))
    def _():
        for cp in weight_copies(e, slot):
            cp.wait()

        @pl.when(nxt_ref[i] >= 0)
        def _():
            for cp in weight_copies(nxt_ref[i], 1 - slot):
                cp.start()

        wgu_bf[...] = wgu_f32[slot].astype(BF16)
        wdn_bf[...] = wdn_f32[slot].astype(BF16)

    @pl.when(used)
    def _():
        gu = jnp.dot(x_ref[...], wgu_bf[...], preferred_element_type=F32) + bgu_ref[0, 0]
        gt = jnp.minimum(gu[:, :D_FF], SWIGLU_LIMIT)
        up = jnp.clip(gu[:, D_FF:], -SWIGLU_LIMIT, SWIGLU_LIMIT)
        act = (up + 1.0) * (gt * jax.nn.sigmoid(gt * SWIGLU_ALPHA))
        y = jnp.dot(act.astype(BF16), wdn_bf[...], preferred_element_type=F32) + bdn_ref[0, 0]
        y_ref[...] = y.astype(y_ref.dtype)

    @pl.when(jnp.logical_not(used))
    def _():
        y_ref[...] = jnp.zeros_like(y_ref)


def moe_experts(layer, block_e, next_e, slot, n_used, buf, w_gu, b_gu, w_dn, b_dn):
    n_rows = buf.shape[0]
    n_blocks = n_rows // MOE_BM
    depth = w_gu.shape[0]
    bias = lambda width: pl.BlockSpec((1, 1, 1, width), lambda i, be, nx, sl, nb: (layer, be[i], 0, 0))
    grid_spec = pltpu.PrefetchScalarGridSpec(
        num_scalar_prefetch=4,
        grid=(n_blocks,),
        in_specs=[
            pl.BlockSpec((MOE_BM, D_MODEL), lambda i, be, nx, sl, nb: (i, 0)),
            pl.BlockSpec(memory_space=pl.ANY),
            bias(2 * D_FF),
            pl.BlockSpec(memory_space=pl.ANY),
            bias(D_MODEL),
        ],
        out_specs=pl.BlockSpec((MOE_BM, D_MODEL), lambda i, be, nx, sl, nb: (i, 0)),
        scratch_shapes=[pltpu.VMEM((2, D_MODEL, 2 * D_FF), F32), pltpu.VMEM((2, D_FF, D_MODEL), F32),
                        pltpu.VMEM((D_MODEL, 2 * D_FF), BF16), pltpu.VMEM((D_FF, D_MODEL), BF16),
                        pltpu.SemaphoreType.DMA((2, 2))],
    )
    return pl.pallas_call(
        functools.partial(_moe_kernel, layer),
        grid_spec=grid_spec,
        out_shape=jax.ShapeDtypeStruct((n_rows, D_MODEL), BF16),
        compiler_params=_cparams(("arbitrary",)),
        name="moe_experts",
    )(block_e, next_e, slot, n_used, buf, w_gu, b_gu.reshape(depth, N_EXPERTS, 1, 2 * D_FF),
      w_dn, b_dn.reshape(depth, N_EXPERTS, 1, D_MODEL))


def moe_ffn(layer, h2, logits, w_gu, b_gu, w_dn, b_dn):
    n_tok = h2.shape[0]
    n_assign = n_tok * TOP_K
    n_blocks = n_assign // MOE_BM + N_EXPERTS
    top_val, top_idx = lax.top_k(logits[:, :N_EXPERTS], TOP_K)
    gates = jax.nn.softmax(top_val, axis=-1)
    e_flat = top_idx.reshape(-1).astype(jnp.int32)
    ar = jnp.arange(n_assign, dtype=jnp.int32)
    e_sorted, order = lax.sort((e_flat, ar), num_keys=1, is_stable=True)
    counts = jnp.sum(e_flat[:, None] == jnp.arange(N_EXPERTS, dtype=jnp.int32)[None, :], axis=0, dtype=jnp.int32)
    padded = (counts + MOE_BM - 1) // MOE_BM * MOE_BM
    pad_end = jnp.cumsum(padded)
    pad_start = pad_end - padded
    start = jnp.cumsum(counts) - counts
    dest_sorted = pad_start[e_sorted] + ar - start[e_sorted]
    _, pos = lax.sort((order, dest_sorted), num_keys=1)
    blk_start = jnp.arange(n_blocks, dtype=jnp.int32) * MOE_BM
    n_used = (pad_end[-1] // MOE_BM).astype(jnp.int32)
    block_e = jnp.sum(pad_end[None, :] <= blk_start[:, None], axis=1, dtype=jnp.int32)
    block_e = jnp.minimum(block_e, N_EXPERTS - 1)
    block_e = jnp.where(jnp.arange(n_blocks) < n_used, block_e, block_e[jnp.maximum(n_used - 1, 0)])
    row = jnp.arange(n_blocks * MOE_BM, dtype=jnp.int32)
    row_e = block_e[row // MOE_BM]
    src = order[jnp.clip(start[row_e] + row - pad_start[row_e], 0, n_assign - 1)] // TOP_K
    e_ids = jnp.arange(N_EXPERTS, dtype=jnp.int32)
    owner = jnp.where(counts > 0, e_ids, N_EXPERTS)
    later = jnp.concatenate([lax.cummin(owner, reverse=True)[1:], jnp.full((1,), N_EXPERTS, jnp.int32)])
    next_of = jnp.where(later < N_EXPERTS, later, -1)
    parity = (jnp.cumsum((counts > 0).astype(jnp.int32)) - 1) % 2
    y_buf = moe_experts(layer, block_e, next_of[block_e], parity[block_e].astype(jnp.int32),
                        n_used.reshape(1), h2[src], w_gu, b_gu, w_dn, b_dn)
    pos_k = pos.reshape(n_tok, TOP_K)
    out = gates[:, 0:1] * y_buf[pos_k[:, 0]].astype(F32)
    for j in range(1, TOP_K):
        out = out + gates[:, j:j + 1] * y_buf[pos_k[:, j]].astype(F32)
    return out


def grid_pos_embed(rows, dim):
    t = jnp.arange(rows * GRID_W)
    r = (t // GRID_W).astype(F32)
    col = (t % GRID_W).astype(F32)
    quarter = dim // 4
    omega = 1.0 / (POS_THETA ** (jnp.arange(quarter, dtype=F32) / quarter))
    er = r[:, None] * omega
    ec = col[:, None] * omega
    return jnp.concatenate([jnp.sin(er), jnp.cos(er), jnp.sin(ec), jnp.cos(ec)], axis=-1)


def _pad_cols(w_in):
    z = lambda n: jnp.zeros(w_in.shape[:-1] + (n,), w_in.dtype)
    return jnp.concatenate([
        w_in[..., :DN_COLS], z(DN_PAD - DN_COLS),
        w_in[..., DN_COLS:DN_COLS + RW_COLS],
        w_in[..., DN_COLS + RW_COLS:], z(GLA_PAD - GLA_COLS)], axis=-1)


def kernel(x_prompt, x_sample, state_delta, state_rwkv, state_gla, c, c_ctx, w_mod, b_mod, norm_mix, norm_ffn, norm_out, w_in, w_out, dn_conv, dn_a_log, dn_dt_bias, dn_norm, rw_mu, rw_w0, rw_w2, rw_a0, rw_a2, rw_g2, rw_k_k, rw_k_a, rw_r_k, rw_ln_w, rw_ln_b, gla_gk2, gla_gk_b, gla_norm, router_w, router_b, moe_w_gu, moe_b_gu, moe_w_dn, moe_b_dn):
    bp, tp, d = x_prompt.shape
    bs, ts, _ = x_sample.shape
    depth = w_in.shape[0]
    n_ctx = bp * tp
    n_lat = bs * ts
    tiles = dict(n_ctx_tiles=n_ctx // ROW_TILE, tiles_per_latent=ts // ROW_TILE)

    cond = jnp.concatenate([c_ctx[None, :], c, jnp.zeros((SUBLANES - 1 - bs, d), F32)], axis=0)
    mod_all = modulation(cond, w_mod, b_mod).reshape(depth, SUBLANES, N_MOD, d)
    mod_all = jnp.pad(mod_all, ((0, 0), (0, 0), (0, SUBLANES - N_MOD), (0, 0)))

    w_in_p = _pad_cols(w_in).astype(BF16)
    w_out_b = w_out.astype(BF16)
    router_w_p = jnp.pad(router_w, ((0, 0), (0, 0), (0, LANES - N_EXPERTS)))
    router_b_p = jnp.pad(router_b, ((0, 0), (0, LANES - N_EXPERTS)))[:, None, :]

    pos = grid_pos_embed(ts // GRID_W, d)
    x = jnp.concatenate([x_prompt.reshape(n_ctx, d), x_sample.reshape(n_lat, d)], axis=0)
    delta = jnp.concatenate([jnp.zeros((n_ctx, d), F32), jnp.tile(pos, (bs, 1))], axis=0)

    st_dn, st_rw, st_gla = [], [], []
    for l in range(depth):
        mod = mod_all[l]
        x, u_dn, u_rw, u_gla = inproj(x, delta, mod_all[max(l - 1, 0)], mod, norm_mix[l][None, :], w_in_p[l],
                                      gate_row=None if l == 0 else 5, **tiles)
        o_dn, o_rw, o_gla = [], [], []
        for (lo, b, t, s_dn, s_rw, s_gla) in (
                (0, bp, tp, None, None, None),
                (n_ctx, bs, ts, state_delta[:, l], state_rwkv[:, l], state_gla[:, l])):
            od, f_dn = deltanet_mixer(u_dn, lo, b, t, dn_conv[l], dn_a_log[l], dn_dt_bias[l], dn_norm[l], s_dn)
            orw, f_rw = rwkv7_mixer(u_rw, lo, b, t, rw_mu[l], rw_w0[l], rw_w2[l], rw_a0[l], rw_a2[l], rw_g2[l],
                                    rw_k_k[l], rw_k_a[l], rw_r_k[l], rw_ln_w[l], rw_ln_b[l], s_rw)
            og, f_gla = gla_mixer(u_gla, lo, b, t, gla_gk2[l], gla_gk_b[l], gla_norm[l], s_gla)
            o_dn.append(od)
            o_rw.append(orw)
            o_gla.append(og)
            if lo == 0:
                st_dn.append(f_dn)
                st_rw.append(f_rw)
                st_gla.append(f_gla)
        x, h2, logits = outproj(jnp.concatenate(o_dn, axis=0), jnp.concatenate(o_rw, axis=0),
                                jnp.concatenate(o_gla, axis=0), x, mod, norm_ffn[l][None, :], w_out_b[l],
                                router_w_p[l], router_b_p[l], **tiles)
        delta = moe_ffn(l, h2, logits, moe_w_gu, moe_b_gu, moe_w_dn, moe_b_dn)
    y = final_norm(x, delta, mod_all[depth - 1], norm_out[None, :], **tiles)
    y_prompt = y[:n_ctx].reshape(bp, tp, d)
    y_sample = y[n_ctx:].reshape(bs, ts, d)
    return (y_prompt, y_sample, jnp.stack(st_dn, axis=1), jnp.stack(st_rw, axis=1), jnp.stack(st_gla, axis=1))
```

```python
import functools

import jax
import jax.numpy as jnp
from jax import lax
from jax.experimental import pallas as pl
from jax.experimental.pallas import tpu as pltpu

F32 = jnp.float32
BF16 = jnp.bfloat16
HIGHEST = lax.Precision.HIGHEST

D_MODEL = 1024
GRID_W = 64
POS_THETA = 10000.0
N_MOD = 6
RMS_EPS = 1e-6

DN_H, DN_DK, DN_DV = 4, 128, 128
DN_W = DN_H * DN_DV
CONV_K = 5

RW_H, RW_DH = 4, 64
RW_W = RW_H * RW_DH
RW_W_LORA, RW_A_LORA, RW_G_LORA = 64, 64, 128
RW_GN_EPS = 64e-5

GLA_H, GLA_DK, GLA_DV = 4, 32, 64
GLA_KW = GLA_H * GLA_DK
GLA_VW = GLA_H * GLA_DV
GLA_GK_LORA = 16
GLA_GATE_NORM = 16.0
GLA_CHUNK = 16

MIX_W = DN_W + RW_W + GLA_VW
DN_COLS = 4 * DN_W + 4 * DN_H
RW_COLS = 3 * RW_W + 2 * RW_W_LORA + 2 * RW_A_LORA + RW_G_LORA
GLA_COLS = 2 * GLA_KW + 2 * GLA_VW + 2 * GLA_GK_LORA

N_EXPERTS = 32
TOP_K = 4
D_FF = 1024
SWIGLU_LIMIT = 7.0
SWIGLU_ALPHA = 1.702

LANES = 128
SUBLANES = 8
VMEM_LIMIT = 56 * 1024 * 1024

DN_PAD = 17 * LANES
RW_PAD = RW_COLS
GLA_PAD = 7 * LANES
U_COLS = DN_PAD + RW_PAD + GLA_PAD

ROW_TILE = 256
MOE_BM = 256
BF16_ROWS = 2 * SUBLANES
MOE_WIN = MOE_BM + LANES
CHUNK = 64
PACK = DN_H * CHUNK
GLA_CHUNKS_PER_STEP = 4
CHUNKS_PER_STEP = 2
N_MERGE = CHUNK.bit_length() - 1
N_MASKS = 4 + N_MERGE
HALO = SUBLANES
NEG_BIG = -1e30


def _cparams(sem):
    return pltpu.CompilerParams(dimension_semantics=sem, vmem_limit_bytes=VMEM_LIMIT)


def _silu(x):
    return x * jax.nn.sigmoid(x)


def _softplus(x):
    return jnp.maximum(x, 0.0) + jnp.log(1.0 + jnp.exp(-jnp.abs(x)))


def _bdot(a, b):
    return jnp.dot(a.astype(BF16), b.astype(BF16), preferred_element_type=F32)


def _bdot_nt(a, b):
    return lax.dot_general(a.astype(BF16), b.astype(BF16), (((1,), (1,)), ((), ())),
                           preferred_element_type=F32)


def _bdot_tn(a, b):
    return lax.dot_general(a.astype(BF16), b.astype(BF16), (((0,), (0,)), ((), ())),
                           preferred_element_type=F32)


def _hdot(a, b):
    return jnp.dot(a, b, precision=HIGHEST, preferred_element_type=F32)


def _mod_kernel(c_ref, w_ref, b_ref, o_ref):
    o_ref[0] = _hdot(_silu(c_ref[...]), w_ref[0]) + b_ref[0]


def modulation(cond, w_mod, b_mod):
    depth = w_mod.shape[0]
    n_out = w_mod.shape[2]
    tn = 1536
    return pl.pallas_call(
        _mod_kernel,
        grid=(depth, n_out // tn),
        in_specs=[
            pl.BlockSpec((SUBLANES, D_MODEL), lambda l, j: (0, 0)),
            pl.BlockSpec((1, D_MODEL, tn), lambda l, j: (l, 0, j)),
            pl.BlockSpec((1, 1, tn), lambda l, j: (l, 0, j)),
        ],
        out_specs=pl.BlockSpec((1, SUBLANES, tn), lambda l, j: (l, 0, j)),
        out_shape=jax.ShapeDtypeStruct((depth, SUBLANES, n_out), F32),
        compiler_params=_cparams(("arbitrary", "arbitrary")),
        name="modulation",
    )(cond, w_mod, b_mod.reshape(depth, 1, n_out))


def _tile_cond(i, n_ctx_tiles, tiles_per_latent):
    return jnp.where(i < n_ctx_tiles, 0, 1 + (i - n_ctx_tiles) // tiles_per_latent)


def _inproj_kernel(gate_row, x_ref, d_ref, mg_ref, mod_ref, g_ref, w_ref, xo_ref, udn_ref, urw_ref, ugla_ref):
    x = x_ref[...]
    if gate_row is None:
        x = x + d_ref[...]
    else:
        x = x + mg_ref[0, gate_row:gate_row + 1, :] * d_ref[...]
    xo_ref[...] = x
    y = x * lax.rsqrt(jnp.mean(x * x, axis=-1, keepdims=True) + RMS_EPS) * g_ref[...]
    h = (y * (1.0 + mod_ref[0, 1:2, :]) + mod_ref[0, 0:1, :]).astype(BF16)
    udn_ref[...] = jnp.dot(h, w_ref[:, :DN_PAD], preferred_element_type=F32)
    urw_ref[...] = jnp.dot(h, w_ref[:, DN_PAD:DN_PAD + RW_PAD], preferred_element_type=F32)
    ugla_ref[...] = jnp.dot(h, w_ref[:, DN_PAD + RW_PAD:], preferred_element_type=F32)


def inproj(x, delta, mod_gate, mod, g, w, n_ctx_tiles, tiles_per_latent, gate_row):
    n = x.shape[0]
    cond_of = functools.partial(_tile_cond, n_ctx_tiles=n_ctx_tiles, tiles_per_latent=tiles_per_latent)
    rows = lambda width: pl.BlockSpec((ROW_TILE, width), lambda i: (i, 0))
    modspec = pl.BlockSpec((1, SUBLANES, D_MODEL), lambda i: (cond_of(i), 0, 0))
    return pl.pallas_call(
        functools.partial(_inproj_kernel, gate_row),
        grid=(n // ROW_TILE,),
        in_specs=[rows(D_MODEL), rows(D_MODEL), modspec, modspec,
                  pl.BlockSpec((1, D_MODEL), lambda i: (0, 0)),
                  pl.BlockSpec((D_MODEL, U_COLS), lambda i: (0, 0))],
        out_specs=[rows(D_MODEL), rows(DN_PAD), rows(RW_PAD), rows(GLA_PAD)],
        out_shape=[jax.ShapeDtypeStruct((n, D_MODEL), F32),
                   jax.ShapeDtypeStruct((n, DN_PAD), F32),
                   jax.ShapeDtypeStruct((n, RW_PAD), F32),
                   jax.ShapeDtypeStruct((n, GLA_PAD), F32)],
        compiler_params=_cparams(("parallel",)),
        name="inproj",
    )(x, delta, mod_gate, mod, g, w)


def _outproj_kernel(odn_ref, orw_ref, ogla_ref, x_ref, mod_ref, g_ref, w_ref, rw_ref, rb_ref,
                    x2_ref, h2_ref, lg_ref):
    mix = (jnp.dot(odn_ref[...].astype(BF16), w_ref[:DN_W, :], preferred_element_type=F32)
           + jnp.dot(orw_ref[...].astype(BF16), w_ref[DN_W:DN_W + RW_W, :], preferred_element_type=F32)
           + jnp.dot(ogla_ref[...].astype(BF16), w_ref[DN_W + RW_W:, :], preferred_element_type=F32))
    x2 = x_ref[...] + mod_ref[0, 2:3, :] * mix
    x2_ref[...] = x2
    y = x2 * lax.rsqrt(jnp.mean(x2 * x2, axis=-1, keepdims=True) + RMS_EPS) * g_ref[...]
    h2 = y * (1.0 + mod_ref[0, 4:5, :]) + mod_ref[0, 3:4, :]
    h2_ref[...] = h2.astype(BF16)
    lg_ref[...] = _hdot(h2, rw_ref[...]) + rb_ref[...]


def outproj(o_dn, o_rw, o_gla, x, mod, g, w, router_w, router_b, n_ctx_tiles, tiles_per_latent):
    n = x.shape[0]
    cond_of = functools.partial(_tile_cond, n_ctx_tiles=n_ctx_tiles, tiles_per_latent=tiles_per_latent)
    rows = lambda width: pl.BlockSpec((ROW_TILE, width), lambda i: (i, 0))
    return pl.pallas_call(
        _outproj_kernel,
        grid=(n // ROW_TILE,),
        in_specs=[rows(DN_W), rows(RW_W), rows(GLA_VW), rows(D_MODEL),
                  pl.BlockSpec((1, SUBLANES, D_MODEL), lambda i: (cond_of(i), 0, 0)),
                  pl.BlockSpec((1, D_MODEL), lambda i: (0, 0)),
                  pl.BlockSpec((MIX_W, D_MODEL), lambda i: (0, 0)),
                  pl.BlockSpec((D_MODEL, LANES), lambda i: (0, 0)),
                  pl.BlockSpec((1, LANES), lambda i: (0, 0))],
        out_specs=[rows(D_MODEL), rows(D_MODEL), rows(LANES)],
        out_shape=[jax.ShapeDtypeStruct((n, D_MODEL), F32),
                   jax.ShapeDtypeStruct((n, D_MODEL), BF16),
                   jax.ShapeDtypeStruct((n, LANES), F32)],
        compiler_params=_cparams(("parallel",)),
        name="outproj",
    )(o_dn, o_rw, o_gla, x, mod, g, w, router_w, router_b)


def _final_kernel(x_ref, d_ref, mod_ref, g_ref, y_ref):
    x = x_ref[...] + mod_ref[0, 5:6, :] * d_ref[...]
    y_ref[...] = x * lax.rsqrt(jnp.mean(x * x, axis=-1, keepdims=True) + RMS_EPS) * g_ref[...]


def final_norm(x, delta, mod, g, n_ctx_tiles, tiles_per_latent):
    n = x.shape[0]
    cond_of = functools.partial(_tile_cond, n_ctx_tiles=n_ctx_tiles, tiles_per_latent=tiles_per_latent)
    rows = pl.BlockSpec((ROW_TILE, D_MODEL), lambda i: (i, 0))
    return pl.pallas_call(
        _final_kernel,
        grid=(n // ROW_TILE,),
        in_specs=[rows, rows,
                  pl.BlockSpec((1, SUBLANES, D_MODEL), lambda i: (cond_of(i), 0, 0)),
                  pl.BlockSpec((1, D_MODEL), lambda i: (0, 0))],
        out_specs=rows,
        out_shape=jax.ShapeDtypeStruct((n, D_MODEL), F32),
        compiler_params=_cparams(("parallel",)),
        name="final_norm",
    )(x, delta, mod, g)


def _chunk_masks(mask_scr):
    r = lax.broadcasted_iota(jnp.int32, (PACK, PACK), 0)
    c = lax.broadcasted_iota(jnp.int32, (PACK, PACK), 1)
    same = (r // CHUNK) == (c // CHUNK)
    tr = r % CHUNK
    tc = c % CHUNK
    mask_scr[0] = jnp.where(same & (tc <= tr), 0.0, NEG_BIG)
    mask_scr[1] = jnp.where(same & (tc >= tr), 0.0, NEG_BIG)
    mask_scr[2] = (same & (tc < tr)).astype(F32)
    mask_scr[3] = (same & (tc > tr)).astype(F32)
    for i in range(N_MERGE):
        m = 1 << i
        mask_scr[4 + i] = (((r // (2 * m)) == (c // (2 * m))) & ((r // m) != (c // m))).astype(F32)


def _cum_matrix(d):
    i = lax.broadcasted_iota(jnp.int32, (CHUNK, CHUNK), 0)
    j = lax.broadcasted_iota(jnp.int32, (CHUNK, CHUNK), 1)
    return ((j <= i) if d == 0 else (j >= i)).astype(F32)


def _tri_inverse(a, mask_scr):
    n = a * mask_scr[4]
    for i in range(1, N_MERGE):
        c = a * mask_scr[4 + i]
        x = c + _bdot(c, n)
        yield
        n = n + x + _bdot(n, x)
        yield
    return n


def _interleave(gens):
    results = [None] * len(gens)
    active = list(range(len(gens)))
    while active:
        for i in list(active):
            try:
                next(gens[i])
            except StopIteration as stop:
                results[i] = stop.value
                active.remove(i)
    return results


def _fill_halo(pad_scr, src_ref, t, width):
    zeros = jnp.zeros((HALO, width), F32)
    pad_scr[0:HALO, :] = zeros
    pad_scr[HALO + t:2 * HALO + t, :] = zeros

    def copy(i, carry):
        r0 = pl.multiple_of(i * CHUNK, CHUNK)
        pad_scr[pl.ds(HALO + r0, CHUNK), :] = src_ref[pl.ds(r0, CHUNK), :width]
        return carry

    lax.fori_loop(0, t // CHUNK, copy, 0)


def _dn_kernel(t, has_s0, u_ref, cw_ref, gp_ref, nw_ref, *rest):
    if has_s0:
        s0_ref, o_ref, sfin_ref = rest[:3]
        scr = rest[3:]
    else:
        o_ref, sfin_ref = rest[:2]
        scr = rest[2:]
    pad_scr, q_scr, k_scr, v_scr, bg_scr, acc_scr, s_scr, mask_scr = scr
    n_chunks = t // CHUNK
    qkv_w = 3 * DN_W

    if has_s0:
        s_scr[...] = s0_ref[0]
    else:
        s_scr[...] = jnp.zeros_like(s_scr)
    acc_scr[...] = jnp.zeros_like(acc_scr)
    _chunk_masks(mask_scr)
    _fill_halo(pad_scr, u_ref, t, qkv_w)

    lane = lax.broadcasted_iota(jnp.int32, (1, LANES), 1)
    neg_a = -jnp.exp(gp_ref[0:1, :])
    dt_b = gp_ref[1:2, :]

    def prep(i, carry):
        r0 = pl.multiple_of(i * CHUNK, CHUNK)
        rows = pl.ds(r0, CHUNK)
        for cg in range(qkv_w // LANES):
            cols = slice(cg * LANES, (cg + 1) * LANES)
            win = pad_scr[pl.ds(r0, CHUNK + 2 * HALO), cols]
            acc = None
            for j in range(CONV_K):
                off = HALO - CONV_K // 2 + j
                term = cw_ref[j:j + 1, cols] * win[off:off + CHUNK, :]
                acc = term if acc is None else acc + term
            y = _silu(acc)
            which, hc = divmod(cg, DN_H)
            hcols = slice(hc * LANES, (hc + 1) * LANES)
            if which < 2:
                y = y * lax.rsqrt(jnp.sum(y * y, axis=-1, keepdims=True) + 1e-6)
            if which == 0:
                q_scr[rows, hcols] = y * DN_DK ** -0.5
            elif which == 1:
                k_scr[rows, hcols] = y
            else:
                v_scr[rows, hcols] = y
        ub = u_ref[rows, 4 * DN_W:4 * DN_W + LANES]
        bg_scr[rows, :] = jnp.where(lane < 2 * DN_H, jax.nn.sigmoid(ub), neg_a * _softplus(ub + dt_b))
        return carry

    lax.fori_loop(0, n_chunks, prep, 0)

    def stack(ref, rows):
        return jnp.concatenate([ref[rows, h * LANES:(h + 1) * LANES] for h in range(DN_H)], axis=0)

    def local(d, rows):
        last = CHUNK - 1 if d == 0 else 0
        bg = bg_scr[rows, :]
        g_cum = _hdot(_cum_matrix(d), bg)
        lanes_g = [2 * DN_H + d * DN_H + h for h in range(DN_H)]
        beta = jnp.concatenate([bg[:, d * DN_H + h:d * DN_H + h + 1] for h in range(DN_H)], axis=0)
        g_col = jnp.concatenate([g_cum[:, l:l + 1] for l in lanes_g], axis=0)
        g_tot = jnp.concatenate(
            [jnp.broadcast_to(g_cum[last:last + 1, l:l + 1], (CHUNK, 1)) for l in lanes_g], axis=0)
        g_row = jnp.transpose(jnp.broadcast_to(g_col, (PACK, LANES)))[0:1, :]
        decay = jnp.exp(g_col - g_row + mask_scr[d])
        qc = stack(q_scr, rows)
        kc = stack(k_scr, rows)
        vc = stack(v_scr, rows)
        qkk = _bdot_nt(jnp.concatenate([qc, kc], axis=0), kc)
        yield
        qk = (qkk[:PACK] * decay).astype(BF16)
        low = (beta * qkk[PACK:]) * (decay * mask_scr[2 + d])
        nm = yield from _tri_inverse(-low, mask_scr)
        eg = jnp.exp(g_col)
        rhs = jnp.concatenate([beta * vc, (beta * eg) * kc], axis=1)
        sol = rhs + _bdot(nm, rhs)
        yield
        wq = [jnp.concatenate([sol[h * CHUNK:(h + 1) * CHUNK, DN_DV:],
                               (qc * eg)[h * CHUNK:(h + 1) * CHUNK]], axis=0).astype(BF16)
              for h in range(DN_H)]
        k_g = (kc * jnp.exp(g_tot - g_col)).astype(BF16)
        return sol[:, :DN_DV], wq, qk, k_g, jnp.exp(g_tot)

    def carried(d, chunk_rows, chunk_parts):
        for rows, parts in zip(chunk_rows, chunk_parts):
            u_t, wq, qk, k_g, e_tot = parts
            us = []
            oq = []
            for h in range(DN_H):
                hr = slice(h * CHUNK, (h + 1) * CHUNK)
                ws = _bdot(wq[h], s_scr[d * DN_H + h])
                us.append(u_t[hr] - ws[:CHUNK])
                oq.append(ws[CHUNK:])
            yield
            u_all = jnp.concatenate(us, axis=0).astype(BF16)
            o_all = jnp.concatenate(oq, axis=0) + _bdot(qk, u_all)
            for h in range(DN_H):
                hr = slice(h * CHUNK, (h + 1) * CHUNK)
                idx = d * DN_H + h
                s_scr[idx] = e_tot[h * CHUNK:h * CHUNK + 1, :] * s_scr[idx] + _bdot_tn(k_g[hr], u_all[hr])
                acc_scr[rows, h * LANES:(h + 1) * LANES] += o_all[hr]
            yield

    def body(n, carry):
        rows = [[], []]
        for j in range(CHUNKS_PER_STEP):
            for d in range(2):
                ch = n * CHUNKS_PER_STEP + j
                ch = ch if d == 0 else n_chunks - 1 - ch
                rows[d].append(pl.ds(pl.multiple_of(ch * CHUNK, CHUNK), CHUNK))
        flat = [(d, r) for d in range(2) for r in rows[d]]
        parts = _interleave([local(d, r) for d, r in flat])
        by_dir = [[p for (dd, _), p in zip(flat, parts) if dd == d] for d in range(2)]
        _interleave([carried(d, rows[d], by_dir[d]) for d in range(2)])
        return carry

    lax.fori_loop(0, n_chunks // CHUNKS_PER_STEP, body, 0)
    sfin_ref[0] = s_scr[...]

    def post(i, carry):
        rows = pl.ds(pl.multiple_of(i * CHUNK, CHUNK), CHUNK)
        for h in range(DN_H):
            cols = slice(h * LANES, (h + 1) * LANES)
            o = acc_scr[rows, cols]
            on = o * lax.rsqrt(jnp.mean(o * o, axis=-1, keepdims=True) + RMS_EPS) * nw_ref[...]
            z = u_ref[rows, 3 * DN_W + h * LANES:3 * DN_W + (h + 1) * LANES]
            o_ref[rows, cols] = on * _silu(z)
        return carry

    lax.fori_loop(0, n_chunks, post, 0)


def deltanet_mixer(u_dn, row0, b, t, conv_w, a_log, dt_bias, norm_w, s0):
    nh2 = 2 * DN_H
    blk0 = row0 // t
    cw = jnp.pad(conv_w, ((0, SUBLANES - CONV_K), (0, 0)))
    gp = jnp.zeros((SUBLANES, LANES), F32)
    gp = gp.at[0, nh2:2 * nh2].set(a_log.reshape(-1)).at[1, nh2:2 * nh2].set(dt_bias.reshape(-1))
    st = pl.BlockSpec((1, nh2, DN_DK, DN_DV), lambda i: (i, 0, 0, 0))
    in_specs = [pl.BlockSpec((t, DN_PAD), lambda i: (blk0 + i, 0)),
                pl.BlockSpec((SUBLANES, 3 * DN_W), lambda i: (0, 0)),
                pl.BlockSpec((SUBLANES, LANES), lambda i: (0, 0)),
                pl.BlockSpec((1, DN_DV), lambda i: (0, 0))]
    args = [u_dn, cw, gp, norm_w[None, :]]
    if s0 is not None:
        in_specs.append(st)
        args.append(s0.reshape(b, nh2, DN_DK, DN_DV))
    o, sfin = pl.pallas_call(
        functools.partial(_dn_kernel, t, s0 is not None),
        grid=(b,),
        in_specs=in_specs,
        out_specs=[pl.BlockSpec((t, DN_W), lambda i: (i, 0)), st],
        out_shape=[jax.ShapeDtypeStruct((b * t, DN_W), F32),
                   jax.ShapeDtypeStruct((b, nh2, DN_DK, DN_DV), F32)],
        scratch_shapes=[pltpu.VMEM((t + 2 * HALO, 3 * DN_W), F32),
                        pltpu.VMEM((t, DN_W), F32), pltpu.VMEM((t, DN_W), F32), pltpu.VMEM((t, DN_W), F32),
                        pltpu.VMEM((t, LANES), F32), pltpu.VMEM((t, DN_W), F32),
                        pltpu.VMEM((nh2, DN_DK, DN_DV), F32),
                        pltpu.VMEM((N_MASKS, PACK, PACK), F32)],
        compiler_params=_cparams(("parallel",)),
        name="deltanet",
    )(*args)
    return o, sfin.reshape(b, 2, DN_H, DN_DK, DN_DV)


def _rw_kernel(t, has_s0, u_ref, mu_ref, w2_ref, a2_ref, g2_ref, wa0_ref, pv_ref, *rest):
    if has_s0:
        s0_ref, o_ref, sfin_ref = rest[:3]
        scr = rest[3:]
    else:
        o_ref, sfin_ref = rest[:2]
        scr = rest[2:]
    pad_scr, r_scr, v_scr, kk_scr, gate_scr, bonus_scr, lw_scr, kd_scr, b_scr, y_scr, z_scr, mask_scr = scr
    n_chunks = t // CHUNK
    w = RW_W

    if has_s0:
        z_scr[...] = s0_ref[0]
    else:
        z_scr[...] = jnp.zeros_like(z_scr)
    y_scr[...] = jnp.zeros_like(y_scr)
    _chunk_masks(mask_scr)
    _fill_halo(pad_scr, u_ref, t, RW_COLS)

    low_half = lax.broadcasted_iota(jnp.int32, (1, LANES), 1) < RW_DH

    def seg_sum(x):
        outs = []
        for gi in range(w // LANES):
            xg = x[:, gi * LANES:(gi + 1) * LANES]
            lo = jnp.sum(jnp.where(low_half, xg, 0.0), axis=-1, keepdims=True)
            hi = jnp.sum(jnp.where(low_half, 0.0, xg), axis=-1, keepdims=True)
            outs.append(jnp.where(low_half, lo, hi))
        return jnp.concatenate(outs, axis=1)

    head_mask = lax.broadcasted_iota(jnp.int32, (PACK, w), 0) // CHUNK == \
        lax.broadcasted_iota(jnp.int32, (PACK, w), 1) // RW_DH

    k_k = pv_ref[0:1, :]
    k_a = pv_ref[1:2, :]
    r_k = pv_ref[2:3, :]
    ln_w = pv_ref[3:4, :]
    ln_b = pv_ref[4:5, :]

    def prep(i, carry):
        r0 = pl.multiple_of(i * CHUNK, CHUNK)
        rows = pl.ds(r0, CHUNK)
        xs = []
        for cg in range(RW_COLS // LANES):
            cols = slice(cg * LANES, (cg + 1) * LANES)
            win = pad_scr[pl.ds(r0, CHUNK + 2 * HALO), cols]
            cur = win[HALO:HALO + CHUNK, :]
            nb = 0.5 * (win[HALO - 1:HALO - 1 + CHUNK, :] + win[HALO + 1:HALO + 1 + CHUNK, :])
            xs.append(cur + mu_ref[:, cols] * (nb - cur))
        x = jnp.concatenate(xs, axis=1)
        r = x[:, :w]
        k = x[:, w:2 * w]
        v = x[:, 2 * w:3 * w]
        wd = x[:, 3 * w:3 * w + 2 * RW_W_LORA]
        ad = x[:, 3 * w + 2 * RW_W_LORA:3 * w + 2 * RW_W_LORA + 2 * RW_A_LORA]
        gd = x[:, 3 * w + 2 * RW_W_LORA + 2 * RW_A_LORA:]
        w_log = -_softplus(-(wa0_ref[0:1, :] + _bdot(jnp.tanh(wd), w2_ref[...]))) - 0.5
        al = jax.nn.sigmoid(wa0_ref[1:2, :] + _bdot(ad, a2_ref[...]))
        kk = k * k_k
        kk = kk * lax.rsqrt(seg_sum(kk * kk) + 1e-6)
        r_scr[rows, :] = r
        v_scr[rows, :] = v
        kk_scr[rows, :] = kk
        gate_scr[rows, :] = _bdot(jax.nn.sigmoid(gd), g2_ref[...])
        bonus_scr[rows, :] = seg_sum(r * k * r_k) * v
        lw_scr[rows, :] = -jnp.exp(w_log)
        for d in range(2):
            dc = slice(d * w, (d + 1) * w)
            kd_scr[rows, dc] = k * (1.0 + (al[:, dc] - 1.0) * k_a)
            b_scr[rows, dc] = kk * al[:, dc]
        return carry

    lax.fori_loop(0, n_chunks, prep, 0)

    def stack(x):
        return jnp.where(head_mask, jnp.concatenate([x] * RW_H, axis=0), 0.0)

    def local(d, rows):
        dc = slice(d * w, (d + 1) * w)
        last = CHUNK - 1 if d == 0 else 0
        lw = lw_scr[rows, dc]
        lc = _hdot(_cum_matrix(d), lw)
        l_tot = lc[last:last + 1, :]
        e_in = jnp.exp(lc)
        e_ex = jnp.exp(lc - lw)
        e_neg = jnp.exp(-lc)
        e_rem = jnp.exp(l_tot - lc)
        kk = kk_scr[rows, :]
        kd = kd_scr[rows, dc]
        bb = b_scr[rows, dc]
        a_t = stack(-kk * e_ex)
        r_t = stack(r_scr[rows, :] * e_in).astype(BF16)
        b_t = stack(bb * e_neg).astype(BF16)
        k_t = stack(kd * e_neg).astype(BF16)
        kb_h = jnp.concatenate([stack(kd * e_rem), stack(bb * e_rem)], axis=0).astype(BF16)
        v_m = stack(v_scr[rows, :]).astype(BF16)
        g = _bdot_nt(jnp.concatenate([a_t.astype(BF16), r_t], axis=0),
                     jnp.concatenate([b_t, k_t], axis=0))
        yield
        strict = mask_scr[2 + d]
        incl = jnp.where(mask_scr[d] == 0.0, 1.0, 0.0)
        a_ab = g[:PACK, :PACK] * strict
        a_k = jnp.concatenate([g[:PACK, PACK:] * strict, g[PACK:, PACK:] * incl], axis=0).astype(BF16)
        a_rb = (g[PACK:, :PACK] * incl).astype(BF16)
        nm = yield from _tri_inverse(a_ab, mask_scr)
        av = _bdot(a_k, v_m)
        yield
        rhs = jnp.concatenate([av[:PACK], a_t], axis=1)
        sol = rhs + _bdot(nm, rhs)
        yield
        ar = jnp.concatenate([sol[:, w:].astype(BF16), r_t], axis=0)
        return sol[:, :w], ar, av[PACK:], a_rb, v_m, kb_h, jnp.exp(l_tot)

    def carried(d, chunk_rows, chunk_parts):
        for rows, parts in zip(chunk_rows, chunk_parts):
            p0, ar, y0, a_rb, v_m, kb_h, e_tot = parts
            zt = z_scr[d]
            az = _bdot_nt(ar, zt)
            yield
            p = (p0 + az[:PACK]).astype(BF16)
            y = y0 + az[PACK:] + _bdot(a_rb, p)
            y_scr[rows, :] += (y[0:CHUNK] + y[CHUNK:2 * CHUNK]) + (y[2 * CHUNK:3 * CHUNK] + y[3 * CHUNK:])
            yield
            z_scr[d] = zt * e_tot + _bdot_tn(jnp.concatenate([v_m, p], axis=0), kb_h)
            yield

    def body(n, carry):
        rows = [[], []]
        for j in range(CHUNKS_PER_STEP):
            for d in range(2):
                ch = n * CHUNKS_PER_STEP + j
                ch = ch if d == 0 else n_chunks - 1 - ch
                rows[d].append(pl.ds(pl.multiple_of(ch * CHUNK, CHUNK), CHUNK))
        flat = [(d, r) for d in range(2) for r in rows[d]]
        parts = _interleave([local(d, r) for d, r in flat])
        by_dir = [[p for (dd, _), p in zip(flat, parts) if dd == d] for d in range(2)]
        _interleave([carried(d, rows[d], by_dir[d]) for d in range(2)])
        return carry

    lax.fori_loop(0, n_chunks // CHUNKS_PER_STEP, body, 0)
    sfin_ref[0] = z_scr[...]

    def post(i, carry):
        rows = pl.ds(pl.multiple_of(i * CHUNK, CHUNK), CHUNK)
        y = y_scr[rows, :]
        m = seg_sum(y) * (1.0 / RW_DH)
        yc = y - m
        var = seg_sum(yc * yc) * (1.0 / RW_DH)
        yn = yc * lax.rsqrt(var + RW_GN_EPS) * ln_w + ln_b
        o_ref[rows, :] = (yn + bonus_scr[rows, :]) * gate_scr[rows, :]
        return carry

    lax.fori_loop(0, n_chunks, post, 0)


def _dir_blocks(m):
    z = jnp.zeros_like(m[0])
    return jnp.concatenate([jnp.concatenate([m[0], z], axis=1), jnp.concatenate([z, m[1]], axis=1)], axis=0)


def rwkv7_mixer(u_rw, row0, b, t, mu, w0, w2, a0, a2, g2, k_k, k_a, r_k, ln_w, ln_b, s0):
    w = RW_W
    blk0 = row0 // t
    w2b = _dir_blocks(w2).astype(BF16)
    a2b = _dir_blocks(a2).astype(BF16)
    wa0 = jnp.zeros((SUBLANES, 2 * w), F32).at[0].set(w0.reshape(-1)).at[1].set(a0.reshape(-1))
    pv = jnp.zeros((SUBLANES, w), F32)
    for i, p in enumerate((k_k, k_a, r_k.reshape(-1), ln_w, ln_b)):
        pv = pv.at[i].set(p)
    full = lambda a: pl.BlockSpec(a.shape, lambda i: (0,) * a.ndim)
    st = pl.BlockSpec((1, 2, w, w), lambda i: (i, 0, 0, 0))
    args = [u_rw, mu[None, :], w2b, a2b, g2.astype(BF16), wa0, pv]
    in_specs = [pl.BlockSpec((t, RW_COLS), lambda i: (blk0 + i, 0))] + [full(a) for a in args[1:]]
    eye_h = jnp.eye(RW_H, dtype=F32)
    if s0 is not None:
        in_specs.append(st)
        args.append(jnp.einsum('bdhvk,hg->bdhvgk', s0, eye_h).reshape(b, 2, w, w))
    seq = lambda width: pltpu.VMEM((t, width), F32)
    o, sfin = pl.pallas_call(
        functools.partial(_rw_kernel, t, s0 is not None),
        grid=(b,),
        in_specs=in_specs,
        out_specs=[pl.BlockSpec((t, w), lambda i: (i, 0)), st],
        out_shape=[jax.ShapeDtypeStruct((b * t, w), F32), jax.ShapeDtypeStruct((b, 2, w, w), F32)],
        scratch_shapes=[pltpu.VMEM((t + 2 * HALO, RW_COLS), F32),
                        seq(w), seq(w), seq(w), seq(w), seq(w), seq(2 * w), seq(2 * w), seq(2 * w), seq(w),
                        pltpu.VMEM((2, w, w), F32), pltpu.VMEM((N_MASKS, PACK, PACK), F32)],
        compiler_params=_cparams(("parallel",)),
        name="rwkv7",
    )(*args)
    sf = sfin.reshape(b, 2, RW_H, RW_DH, RW_H, RW_DH)
    return o, jnp.einsum('bdhvhk->bdhvk', sf)


def _gla_kernel(t, has_s0, u_ref, gk2_ref, gkb_ref, nw_ref, *rest):
    if has_s0:
        s0_ref, o_ref, sfin_ref, s_scr, gk_scr, acc_scr = rest
        s_scr[...] = s0_ref[0]
    else:
        o_ref, sfin_ref, s_scr, gk_scr, acc_scr = rest
        s_scr[...] = jnp.zeros_like(s_scr)
    c = GLA_CHUNK
    n_chunks = t // c
    acc_scr[...] = jnp.zeros_like(acc_scr)
    q_cols = slice(0, GLA_KW)
    k_cols = slice(GLA_KW, 2 * GLA_KW)
    v_cols = slice(2 * GLA_KW, 2 * GLA_KW + GLA_VW)
    g_off = 2 * GLA_KW + GLA_VW
    lora_cols = slice(g_off + GLA_VW, g_off + GLA_VW + LANES)

    def prep(i, carry):
        rows = pl.ds(pl.multiple_of(i * CHUNK, CHUNK), CHUNK)
        z = _bdot(u_ref[rows, lora_cols], gk2_ref[...]) + gkb_ref[...]
        gk_scr[rows, :] = -_softplus(-z) * (1.0 / GLA_GATE_NORM)
        return carry

    lax.fori_loop(0, t // CHUNK, prep, 0)

    ii = lax.broadcasted_iota(jnp.int32, (c, c), 0)
    jj = lax.broadcasted_iota(jnp.int32, (c, c), 1)
    cum_m = ((ii >= jj).astype(F32), (ii <= jj).astype(F32))
    row_i = lax.broadcasted_iota(jnp.int32, (c, 1), 0)
    hk = lax.broadcasted_iota(jnp.int32, (GLA_KW, GLA_VW), 0) // GLA_DK
    hv = lax.broadcasted_iota(jnp.int32, (GLA_KW, GLA_VW), 1) // GLA_DV
    head_sum = (hk == hv).astype(BF16)
    bd_mask = (lax.broadcasted_iota(jnp.int32, (GLA_VW, GLA_KW), 0) // GLA_DV
               == lax.broadcasted_iota(jnp.int32, (GLA_VW, GLA_KW), 1) // GLA_DK)

    def local(d, rows):
        qc = u_ref[rows, q_cols] * GLA_DK ** -0.5
        kc = u_ref[rows, k_cols]
        vc = u_ref[rows, v_cols]
        g = gk_scr[rows, d * GLA_KW:(d + 1) * GLA_KW]
        bc = _hdot(cum_m[d], g)
        yield
        last = c - 1 if d == 0 else 0
        bc_last = bc[last:last + 1, :]
        terms = []
        for j in range(c):
            keep = (row_i >= j) if d == 0 else (row_i <= j)
            dec = jnp.exp(jnp.where(keep, bc - bc[j:j + 1, :], NEG_BIG))
            terms.append(qc * kc[j:j + 1, :] * dec)
        t_all = jnp.concatenate(terms, axis=0).astype(BF16)
        att = jnp.dot(t_all, head_sum, preferred_element_type=F32)
        yield
        o = None
        for j in range(c):
            term = att[j * c:(j + 1) * c, :] * vc[j:j + 1, :]
            o = term if o is None else o + term
        q_g = (qc * jnp.exp(bc)).astype(BF16)
        upd = jnp.where(bd_mask, _bdot_tn(vc, kc * jnp.exp(bc_last - bc)), 0.0)
        yield
        return o, q_g, upd, jnp.exp(bc_last)

    def carried(d, chunk_rows, chunk_parts):
        for rows, parts in zip(chunk_rows, chunk_parts):
            o, q_g, upd, e_last = parts
            st = s_scr[d]
            acc_scr[rows, :] += o + _bdot_nt(q_g, st)
            s_scr[d] = st * e_last + upd
            yield

    def body(n, carry):
        rows = [[], []]
        for j in range(GLA_CHUNKS_PER_STEP):
            for d in range(2):
                ch = n * GLA_CHUNKS_PER_STEP + j
                ch = ch if d == 0 else n_chunks - 1 - ch
                rows[d].append(pl.ds(pl.multiple_of(ch * c, c), c))
        flat = [(d, r) for d in range(2) for r in rows[d]]
        parts = _interleave([local(d, r) for d, r in flat])
        by_dir = [[p for (dd, _), p in zip(flat, parts) if dd == d] for d in range(2)]
        _interleave([carried(d, rows[d], by_dir[d]) for d in range(2)])
        return carry

    lax.fori_loop(0, n_chunks // GLA_CHUNKS_PER_STEP, body, 0)
    sfin_ref[0] = s_scr[...]

    low_half = lax.broadcasted_iota(jnp.int32, (1, LANES), 1) < GLA_DV

    def post(i, carry):
        rows = pl.ds(pl.multiple_of(i * CHUNK, CHUNK), CHUNK)
        for gi in range(GLA_VW // LANES):
            cols = slice(gi * LANES, (gi + 1) * LANES)
            o = acc_scr[rows, cols]
            sq = o * o
            lo = jnp.sum(jnp.where(low_half, sq, 0.0), axis=-1, keepdims=True)
            hi = jnp.sum(jnp.where(low_half, 0.0, sq), axis=-1, keepdims=True)
            ms = jnp.where(low_half, lo, hi) * (1.0 / GLA_DV)
            gate = u_ref[rows, g_off + gi * LANES:g_off + (gi + 1) * LANES]
            o_ref[rows, cols] = o * lax.rsqrt(ms + RMS_EPS) * nw_ref[:, cols] * _silu(gate)
        return carry

    lax.fori_loop(0, t // CHUNK, post, 0)


def gla_mixer(u_gla, row0, b, t, gk2, gk_b, norm_w, s0):
    blk0 = row0 // t
    gk2b = jnp.pad(_dir_blocks(gk2), ((0, LANES - 2 * GLA_GK_LORA), (0, 0))).astype(BF16)
    st = pl.BlockSpec((1, 2, GLA_VW, GLA_KW), lambda i: (i, 0, 0, 0))
    in_specs = [pl.BlockSpec((t, GLA_PAD), lambda i: (blk0 + i, 0)),
                pl.BlockSpec((LANES, 2 * GLA_KW), lambda i: (0, 0)),
                pl.BlockSpec((1, 2 * GLA_KW), lambda i: (0, 0)),
                pl.BlockSpec((1, GLA_VW), lambda i: (0, 0))]
    args = [u_gla, gk2b, gk_b.reshape(1, 2 * GLA_KW), jnp.tile(norm_w, GLA_H)[None, :]]
    if s0 is not None:
        eye_h = jnp.eye(GLA_H, dtype=F32)
        in_specs.append(st)
        args.append(jnp.einsum('bdhkv,hg->bdhvgk', s0, eye_h).reshape(b, 2, GLA_VW, GLA_KW))
    o, sfin_t = pl.pallas_call(
        functools.partial(_gla_kernel, t, s0 is not None),
        grid=(b,),
        in_specs=in_specs,
        out_specs=[pl.BlockSpec((t, GLA_VW), lambda i: (i, 0)), st],
        out_shape=[jax.ShapeDtypeStruct((b * t, GLA_VW), F32),
                   jax.ShapeDtypeStruct((b, 2, GLA_VW, GLA_KW), F32)],
        scratch_shapes=[pltpu.VMEM((2, GLA_VW, GLA_KW), F32), pltpu.VMEM((t, 2 * GLA_KW), F32),
                        pltpu.VMEM((t, GLA_VW), F32)],
        compiler_params=_cparams(("parallel",)),
        name="gla",
    )(*args)
    sf = sfin_t.reshape(b, 2, GLA_H, GLA_DV, GLA_H, GLA_DK)
    return o, jnp.einsum('bdhvhk->bdhkv', sf)


def _moe_kernel(layer, be_ref, nxt_ref, slot_ref, rs_ref, nb_ref, xs_hbm, wgu_hbm, bgu_ref, wdn_hbm, bdn_ref,
                y_ref, wgu_f32, wdn_f32, wgu_bf, wdn_bf, xwin, sem, xsem):
    i = pl.program_id(0)
    used = i < nb_ref[0]
    e = be_ref[i]
    prev = be_ref[jnp.maximum(i - 1, 0)]
    fresh = jnp.logical_or(i == 0, e != prev)
    slot = slot_ref[i]

    def weight_copies(expert, s):
        return (pltpu.make_async_copy(wgu_hbm.at[layer, expert], wgu_f32.at[s], sem.at[0, s]),
                pltpu.make_async_copy(wdn_hbm.at[layer, expert], wdn_f32.at[s], sem.at[1, s]))

    def window_copy(blk):
        first = pl.multiple_of(rs_ref[blk] // BF16_ROWS * BF16_ROWS, BF16_ROWS)
        s = blk % 2
        return pltpu.make_async_copy(xs_hbm.at[pl.ds(first, MOE_WIN), :], xwin.at[s], xsem.at[s])

    @pl.when(jnp.logical_and(used, i == 0))
    def _():
        window_copy(i).start()
        for cp in weight_copies(e, slot):
            cp.start()

    @pl.when(i + 1 < nb_ref[0])
    def _():
        window_copy(i + 1).start()

    @pl.when(jnp.logical_and(used, fresh))
    def _():
        for cp in weight_copies(e, slot):
            cp.wait()

        @pl.when(nxt_ref[i] >= 0)
        def _():
            for cp in weight_copies(nxt_ref[i], 1 - slot):
                cp.start()

        wgu_bf[...] = wgu_f32[slot].astype(BF16)
        wdn_bf[...] = wdn_f32[slot].astype(BF16)

    @pl.when(used)
    def _():
        window_copy(i).wait()
        shift = rs_ref[i] % BF16_ROWS
        r = lax.broadcasted_iota(jnp.int32, (MOE_BM, MOE_WIN), 0)
        c = lax.broadcasted_iota(jnp.int32, (MOE_BM, MOE_WIN), 1)
        pick = (c == r + shift).astype(BF16)
        x = jnp.dot(pick, xwin[i % 2], preferred_element_type=F32).astype(BF16)
        gu = jnp.dot(x, wgu_bf[...], preferred_element_type=F32) + bgu_ref[0, 0]
        gt = jnp.minimum(gu[:, :D_FF], SWIGLU_LIMIT)
        up = jnp.clip(gu[:, D_FF:], -SWIGLU_LIMIT, SWIGLU_LIMIT)
        act = (up + 1.0) * (gt * jax.nn.sigmoid(gt * SWIGLU_ALPHA))
        y = jnp.dot(act.astype(BF16), wdn_bf[...], preferred_element_type=F32) + bdn_ref[0, 0]
        y_ref[...] = y.astype(y_ref.dtype)

    @pl.when(jnp.logical_not(used))
    def _():
        y_ref[...] = jnp.zeros_like(y_ref)


def moe_experts(layer, block_e, next_e, slot, row_start, n_used, xs, w_gu, b_gu, w_dn, b_dn):
    n_blocks = block_e.shape[0]
    depth = w_gu.shape[0]
    bias = lambda width: pl.BlockSpec((1, 1, 1, width), lambda i, be, nx, sl, rs, nb: (layer, be[i], 0, 0))
    grid_spec = pltpu.PrefetchScalarGridSpec(
        num_scalar_prefetch=5,
        grid=(n_blocks,),
        in_specs=[
            pl.BlockSpec(memory_space=pl.ANY),
            pl.BlockSpec(memory_space=pl.ANY),
            bias(2 * D_FF),
            pl.BlockSpec(memory_space=pl.ANY),
            bias(D_MODEL),
        ],
        out_specs=pl.BlockSpec((MOE_BM, D_MODEL), lambda i, be, nx, sl, rs, nb: (i, 0)),
        scratch_shapes=[pltpu.VMEM((2, D_MODEL, 2 * D_FF), F32), pltpu.VMEM((2, D_FF, D_MODEL), F32),
                        pltpu.VMEM((D_MODEL, 2 * D_FF), BF16), pltpu.VMEM((D_FF, D_MODEL), BF16),
                        pltpu.VMEM((2, MOE_WIN, D_MODEL), BF16),
                        pltpu.SemaphoreType.DMA((2, 2)), pltpu.SemaphoreType.DMA((2,))],
    )
    return pl.pallas_call(
        functools.partial(_moe_kernel, layer),
        grid_spec=grid_spec,
        out_shape=jax.ShapeDtypeStruct((n_blocks * MOE_BM, D_MODEL), BF16),
        compiler_params=_cparams(("arbitrary",)),
        name="moe_experts",
    )(block_e, next_e, slot, row_start, n_used, xs, w_gu, b_gu.reshape(depth, N_EXPERTS, 1, 2 * D_FF),
      w_dn, b_dn.reshape(depth, N_EXPERTS, 1, D_MODEL))


def moe_ffn(layer, h2, logits, w_gu, b_gu, w_dn, b_dn):
    n_tok = h2.shape[0]
    n_assign = n_tok * TOP_K
    n_blocks = n_assign // MOE_BM + N_EXPERTS
    top_val, top_idx = lax.top_k(logits[:, :N_EXPERTS], TOP_K)
    gates = jax.nn.softmax(top_val, axis=-1)
    e_flat = top_idx.reshape(-1).astype(jnp.int32)
    ar = jnp.arange(n_assign, dtype=jnp.int32)
    e_sorted, order = lax.sort((e_flat, ar), num_keys=1, is_stable=True)
    counts = jnp.sum(e_flat[:, None] == jnp.arange(N_EXPERTS, dtype=jnp.int32)[None, :], axis=0, dtype=jnp.int32)
    padded = (counts + MOE_BM - 1) // MOE_BM * MOE_BM
    pad_end = jnp.cumsum(padded)
    pad_start = pad_end - padded
    start = jnp.cumsum(counts) - counts
    dest_sorted = ar + (pad_start - start)[e_sorted]
    _, pos = lax.sort((order, dest_sorted), num_keys=1)
    blk_start = jnp.arange(n_blocks, dtype=jnp.int32) * MOE_BM
    n_used = (pad_end[-1] // MOE_BM).astype(jnp.int32)
    block_e = jnp.sum(pad_end[None, :] <= blk_start[:, None], axis=1, dtype=jnp.int32)
    block_e = jnp.minimum(block_e, N_EXPERTS - 1)
    block_e = jnp.where(jnp.arange(n_blocks) < n_used, block_e, block_e[jnp.maximum(n_used - 1, 0)])
    xs = h2[jnp.concatenate([order // TOP_K, jnp.zeros((MOE_WIN,), jnp.int32)])]
    row_start = jnp.clip(start[block_e] + blk_start - pad_start[block_e], 0, n_assign - 1)
    e_ids = jnp.arange(N_EXPERTS, dtype=jnp.int32)
    owner = jnp.where(counts > 0, e_ids, N_EXPERTS)
    later = jnp.concatenate([lax.cummin(owner, reverse=True)[1:], jnp.full((1,), N_EXPERTS, jnp.int32)])
    next_of = jnp.where(later < N_EXPERTS, later, -1)
    parity = (jnp.cumsum((counts > 0).astype(jnp.int32)) - 1) % 2
    y_buf = moe_experts(layer, block_e, next_of[block_e], parity[block_e].astype(jnp.int32), row_start,
                        n_used.reshape(1), xs, w_gu, b_gu, w_dn, b_dn)
    pos_k = pos.reshape(n_tok, TOP_K)
    out = gates[:, 0:1] * y_buf[pos_k[:, 0]].astype(F32)
    for j in range(1, TOP_K):
        out = out + gates[:, j:j + 1] * y_buf[pos_k[:, j]].astype(F32)
    return out


def grid_pos_embed(rows, dim):
    t = jnp.arange(rows * GRID_W)
    r = (t // GRID_W).astype(F32)
    col = (t % GRID_W).astype(F32)
    quarter = dim // 4
    omega = 1.0 / (POS_THETA ** (jnp.arange(quarter, dtype=F32) / quarter))
    er = r[:, None] * omega
    ec = col[:, None] * omega
    return jnp.concatenate([jnp.sin(er), jnp.cos(er), jnp.sin(ec), jnp.cos(ec)], axis=-1)


def _pad_cols(w_in):
    z = lambda n: jnp.zeros(w_in.shape[:-1] + (n,), w_in.dtype)
    return jnp.concatenate([
        w_in[..., :DN_COLS], z(DN_PAD - DN_COLS),
        w_in[..., DN_COLS:DN_COLS + RW_COLS],
        w_in[..., DN_COLS + RW_COLS:], z(GLA_PAD - GLA_COLS)], axis=-1)


def kernel(x_prompt, x_sample, state_delta, state_rwkv, state_gla, c, c_ctx, w_mod, b_mod, norm_mix, norm_ffn, norm_out, w_in, w_out, dn_conv, dn_a_log, dn_dt_bias, dn_norm, rw_mu, rw_w0, rw_w2, rw_a0, rw_a2, rw_g2, rw_k_k, rw_k_a, rw_r_k, rw_ln_w, rw_ln_b, gla_gk2, gla_gk_b, gla_norm, router_w, router_b, moe_w_gu, moe_b_gu, moe_w_dn, moe_b_dn):
    bp, tp, d = x_prompt.shape
    bs, ts, _ = x_sample.shape
    depth = w_in.shape[0]
    n_ctx = bp * tp
    n_lat = bs * ts
    tiles = dict(n_ctx_tiles=n_ctx // ROW_TILE, tiles_per_latent=ts // ROW_TILE)

    cond = jnp.concatenate([c_ctx[None, :], c, jnp.zeros((SUBLANES - 1 - bs, d), F32)], axis=0)
    mod_all = modulation(cond, w_mod, b_mod).reshape(depth, SUBLANES, N_MOD, d)
    mod_all = jnp.pad(mod_all, ((0, 0), (0, 0), (0, SUBLANES - N_MOD), (0, 0)))

    w_in_p = _pad_cols(w_in).astype(BF16)
    w_out_b = w_out.astype(BF16)
    router_w_p = jnp.pad(router_w, ((0, 0), (0, 0), (0, LANES - N_EXPERTS)))
    router_b_p = jnp.pad(router_b, ((0, 0), (0, LANES - N_EXPERTS)))[:, None, :]

    pos = grid_pos_embed(ts // GRID_W, d)
    x = jnp.concatenate([x_prompt.reshape(n_ctx, d), x_sample.reshape(n_lat, d)], axis=0)
    delta = jnp.concatenate([jnp.zeros((n_ctx, d), F32), jnp.tile(pos, (bs, 1))], axis=0)

    st_dn, st_rw, st_gla = [], [], []
    for l in range(depth):
        mod = mod_all[l]
        x, u_dn, u_rw, u_gla = inproj(x, delta, mod_all[max(l - 1, 0)], mod, norm_mix[l][None, :], w_in_p[l],
                                      gate_row=None if l == 0 else 5, **tiles)
        o_dn, o_rw, o_gla = [], [], []
        for (lo, b, t, s_dn, s_rw, s_gla) in (
                (0, bp, tp, None, None, None),
                (n_ctx, bs, ts, state_delta[:, l], state_rwkv[:, l], state_gla[:, l])):
            od, f_dn = deltanet_mixer(u_dn, lo, b, t, dn_conv[l], dn_a_log[l], dn_dt_bias[l], dn_norm[l], s_dn)
            orw, f_rw = rwkv7_mixer(u_rw, lo, b, t, rw_mu[l], rw_w0[l], rw_w2[l], rw_a0[l], rw_a2[l], rw_g2[l],
                                    rw_k_k[l], rw_k_a[l], rw_r_k[l], rw_ln_w[l], rw_ln_b[l], s_rw)
            og, f_gla = gla_mixer(u_gla, lo, b, t, gla_gk2[l], gla_gk_b[l], gla_norm[l], s_gla)
            o_dn.append(od)
            o_rw.append(orw)
            o_gla.append(og)
            if lo == 0:
                st_dn.append(f_dn)
                st_rw.append(f_rw)
                st_gla.append(f_gla)
        x, h2, logits = outproj(jnp.concatenate(o_dn, axis=0), jnp.concatenate(o_rw, axis=0),
                                jnp.concatenate(o_gla, axis=0), x, mod, norm_ffn[l][None, :], w_out_b[l],
                                router_w_p[l], router_b_p[l], **tiles)
        delta = moe_ffn(l, h2, logits, moe_w_gu, moe_b_gu, moe_w_dn, moe_b_dn)
    y = final_norm(x, delta, mod_all[depth - 1], norm_out[None, :], **tiles)
    y_prompt = y[:n_ctx].reshape(bp, tp, d)
    y_sample = y[n_ctx:].reshape(bs, ts, d)
    return (y_prompt, y_sample, jnp.stack(st_dn, axis=1), jnp.stack(st_rw, axis=1), jnp.stack(st_gla, axis=1))
```

```python
import functools

import jax
import jax.numpy as jnp
from jax import lax
from jax.experimental import pallas as pl
from jax.experimental.pallas import tpu as pltpu

F32 = jnp.float32
BF16 = jnp.bfloat16
HIGHEST = lax.Precision.HIGHEST

D_MODEL = 1024
GRID_W = 64
POS_THETA = 10000.0
N_MOD = 6
RMS_EPS = 1e-6

DN_H, DN_DK, DN_DV = 4, 128, 128
DN_W = DN_H * DN_DV
CONV_K = 5

RW_H, RW_DH = 4, 64
RW_W = RW_H * RW_DH
RW_W_LORA, RW_A_LORA, RW_G_LORA = 64, 64, 128
RW_GN_EPS = 64e-5

GLA_H, GLA_DK, GLA_DV = 4, 32, 64
GLA_KW = GLA_H * GLA_DK
GLA_VW = GLA_H * GLA_DV
GLA_GK_LORA = 16
GLA_GATE_NORM = 16.0
GLA_CHUNK = 16

MIX_W = DN_W + RW_W + GLA_VW
DN_COLS = 4 * DN_W + 4 * DN_H
RW_COLS = 3 * RW_W + 2 * RW_W_LORA + 2 * RW_A_LORA + RW_G_LORA
GLA_COLS = 2 * GLA_KW + 2 * GLA_VW + 2 * GLA_GK_LORA

N_EXPERTS = 32
TOP_K = 4
D_FF = 1024
SWIGLU_LIMIT = 7.0
SWIGLU_ALPHA = 1.702

LANES = 128
SUBLANES = 8
VMEM_LIMIT = 56 * 1024 * 1024

DN_PAD = 17 * LANES
RW_PAD = RW_COLS
GLA_PAD = 7 * LANES
U_COLS = DN_PAD + RW_PAD + GLA_PAD

ROW_TILE = 256
MOE_BM = 256
BF16_ROWS = 2 * SUBLANES
MOE_WIN = MOE_BM + LANES
CHUNK = 64
PACK = DN_H * CHUNK
GLA_CHUNKS_PER_STEP = 4
CHUNKS_PER_STEP = 2
N_MERGE = CHUNK.bit_length() - 1
N_MASKS = 4 + N_MERGE
HALO = SUBLANES
NEG_BIG = -1e30


def _cparams(sem):
    return pltpu.CompilerParams(dimension_semantics=sem, vmem_limit_bytes=VMEM_LIMIT)


def _silu(x):
    return x * jax.nn.sigmoid(x)


def _softplus(x):
    return jnp.maximum(x, 0.0) + jnp.log(1.0 + jnp.exp(-jnp.abs(x)))


def _bdot(a, b):
    return jnp.dot(a.astype(BF16), b.astype(BF16), preferred_element_type=F32)


def _bdot_nt(a, b):
    return lax.dot_general(a.astype(BF16), b.astype(BF16), (((1,), (1,)), ((), ())),
                           preferred_element_type=F32)


def _bdot_tn(a, b):
    return lax.dot_general(a.astype(BF16), b.astype(BF16), (((0,), (0,)), ((), ())),
                           preferred_element_type=F32)


def _hdot(a, b):
    return jnp.dot(a, b, precision=HIGHEST, preferred_element_type=F32)


def _mod_kernel(c_ref, w_ref, b_ref, o_ref):
    o_ref[0] = _hdot(_silu(c_ref[...]), w_ref[0]) + b_ref[0]


def modulation(cond, w_mod, b_mod):
    depth = w_mod.shape[0]
    n_out = w_mod.shape[2]
    tn = 1536
    return pl.pallas_call(
        _mod_kernel,
        grid=(depth, n_out // tn),
        in_specs=[
            pl.BlockSpec((SUBLANES, D_MODEL), lambda l, j: (0, 0)),
            pl.BlockSpec((1, D_MODEL, tn), lambda l, j: (l, 0, j)),
            pl.BlockSpec((1, 1, tn), lambda l, j: (l, 0, j)),
        ],
        out_specs=pl.BlockSpec((1, SUBLANES, tn), lambda l, j: (l, 0, j)),
        out_shape=jax.ShapeDtypeStruct((depth, SUBLANES, n_out), F32),
        compiler_params=_cparams(("arbitrary", "arbitrary")),
        name="modulation",
    )(cond, w_mod, b_mod.reshape(depth, 1, n_out))


def _tile_cond(i, n_ctx_tiles, tiles_per_latent):
    return jnp.where(i < n_ctx_tiles, 0, 1 + (i - n_ctx_tiles) // tiles_per_latent)


def _inproj_kernel(gate_row, x_ref, d_ref, mg_ref, mod_ref, g_ref, w_ref, xo_ref, udn_ref, urw_ref, ugla_ref):
    x = x_ref[...]
    if gate_row is None:
        x = x + d_ref[...]
    else:
        x = x + mg_ref[0, gate_row:gate_row + 1, :] * d_ref[...]
    xo_ref[...] = x
    y = x * lax.rsqrt(jnp.mean(x * x, axis=-1, keepdims=True) + RMS_EPS) * g_ref[...]
    h = (y * (1.0 + mod_ref[0, 1:2, :]) + mod_ref[0, 0:1, :]).astype(BF16)
    udn_ref[...] = jnp.dot(h, w_ref[:, :DN_PAD], preferred_element_type=F32)
    urw_ref[...] = jnp.dot(h, w_ref[:, DN_PAD:DN_PAD + RW_PAD], preferred_element_type=F32)
    ugla_ref[...] = jnp.dot(h, w_ref[:, DN_PAD + RW_PAD:], preferred_element_type=F32)


def inproj(x, delta, mod_gate, mod, g, w, n_ctx_tiles, tiles_per_latent, gate_row):
    n = x.shape[0]
    cond_of = functools.partial(_tile_cond, n_ctx_tiles=n_ctx_tiles, tiles_per_latent=tiles_per_latent)
    rows = lambda width: pl.BlockSpec((ROW_TILE, width), lambda i: (i, 0))
    modspec = pl.BlockSpec((1, SUBLANES, D_MODEL), lambda i: (cond_of(i), 0, 0))
    return pl.pallas_call(
        functools.partial(_inproj_kernel, gate_row),
        grid=(n // ROW_TILE,),
        in_specs=[rows(D_MODEL), rows(D_MODEL), modspec, modspec,
                  pl.BlockSpec((1, D_MODEL), lambda i: (0, 0)),
                  pl.BlockSpec((D_MODEL, U_COLS), lambda i: (0, 0))],
        out_specs=[rows(D_MODEL), rows(DN_PAD), rows(RW_PAD), rows(GLA_PAD)],
        out_shape=[jax.ShapeDtypeStruct((n, D_MODEL), F32),
                   jax.ShapeDtypeStruct((n, DN_PAD), F32),
                   jax.ShapeDtypeStruct((n, RW_PAD), F32),
                   jax.ShapeDtypeStruct((n, GLA_PAD), F32)],
        compiler_params=_cparams(("parallel",)),
        name="inproj",
    )(x, delta, mod_gate, mod, g, w)


def _outproj_kernel(odn_ref, orw_ref, ogla_ref, x_ref, mod_ref, g_ref, w_ref, rw_ref, rb_ref,
                    x2_ref, h2_ref, idx_ref, gate_ref):
    mix = (jnp.dot(odn_ref[...].astype(BF16), w_ref[:DN_W, :], preferred_element_type=F32)
           + jnp.dot(orw_ref[...].astype(BF16), w_ref[DN_W:DN_W + RW_W, :], preferred_element_type=F32)
           + jnp.dot(ogla_ref[...].astype(BF16), w_ref[DN_W + RW_W:, :], preferred_element_type=F32))
    x2 = x_ref[...] + mod_ref[0, 2:3, :] * mix
    x2_ref[...] = x2
    y = x2 * lax.rsqrt(jnp.mean(x2 * x2, axis=-1, keepdims=True) + RMS_EPS) * g_ref[...]
    h2 = y * (1.0 + mod_ref[0, 4:5, :]) + mod_ref[0, 3:4, :]
    h2_ref[...] = h2.astype(BF16)
    logits = _hdot(h2, rw_ref[...]) + rb_ref[...]
    lane = lax.broadcasted_iota(jnp.int32, logits.shape, 1).astype(F32)
    vals = jnp.where(lane < N_EXPERTS, logits, -jnp.inf)
    tops = []
    idx_out = jnp.zeros_like(logits)
    for j in range(TOP_K):
        m = jnp.max(vals, axis=-1, keepdims=True)
        pick = jnp.min(jnp.where(vals == m, lane, float(LANES)), axis=-1, keepdims=True)
        vals = jnp.where(lane == pick, -jnp.inf, vals)
        idx_out = jnp.where(lane == j, pick, idx_out)
        tops.append(m)
    exps = [jnp.exp(m - tops[0]) for m in tops]
    inv = 1.0 / sum(exps[1:], exps[0])
    gate_out = jnp.zeros_like(logits)
    for j in range(TOP_K):
        gate_out = jnp.where(lane == j, exps[j] * inv, gate_out)
    idx_ref[...] = idx_out.astype(jnp.int32)
    gate_ref[...] = gate_out


def outproj(o_dn, o_rw, o_gla, x, mod, g, w, router_w, router_b, n_ctx_tiles, tiles_per_latent):
    n = x.shape[0]
    cond_of = functools.partial(_tile_cond, n_ctx_tiles=n_ctx_tiles, tiles_per_latent=tiles_per_latent)
    rows = lambda width: pl.BlockSpec((ROW_TILE, width), lambda i: (i, 0))
    return pl.pallas_call(
        _outproj_kernel,
        grid=(n // ROW_TILE,),
        in_specs=[rows(DN_W), rows(RW_W), rows(GLA_VW), rows(D_MODEL),
                  pl.BlockSpec((1, SUBLANES, D_MODEL), lambda i: (cond_of(i), 0, 0)),
                  pl.BlockSpec((1, D_MODEL), lambda i: (0, 0)),
                  pl.BlockSpec((MIX_W, D_MODEL), lambda i: (0, 0)),
                  pl.BlockSpec((D_MODEL, LANES), lambda i: (0, 0)),
                  pl.BlockSpec((1, LANES), lambda i: (0, 0))],
        out_specs=[rows(D_MODEL), rows(D_MODEL), rows(LANES), rows(LANES)],
        out_shape=[jax.ShapeDtypeStruct((n, D_MODEL), F32),
                   jax.ShapeDtypeStruct((n, D_MODEL), BF16),
                   jax.ShapeDtypeStruct((n, LANES), jnp.int32),
                   jax.ShapeDtypeStruct((n, LANES), F32)],
        compiler_params=_cparams(("parallel",)),
        name="outproj",
    )(o_dn, o_rw, o_gla, x, mod, g, w, router_w, router_b)


def _final_kernel(x_ref, d_ref, mod_ref, g_ref, y_ref):
    x = x_ref[...] + mod_ref[0, 5:6, :] * d_ref[...]
    y_ref[...] = x * lax.rsqrt(jnp.mean(x * x, axis=-1, keepdims=True) + RMS_EPS) * g_ref[...]


def final_norm(x, delta, mod, g, n_ctx_tiles, tiles_per_latent):
    n = x.shape[0]
    cond_of = functools.partial(_tile_cond, n_ctx_tiles=n_ctx_tiles, tiles_per_latent=tiles_per_latent)
    rows = pl.BlockSpec((ROW_TILE, D_MODEL), lambda i: (i, 0))
    return pl.pallas_call(
        _final_kernel,
        grid=(n // ROW_TILE,),
        in_specs=[rows, rows,
                  pl.BlockSpec((1, SUBLANES, D_MODEL), lambda i: (cond_of(i), 0, 0)),
                  pl.BlockSpec((1, D_MODEL), lambda i: (0, 0))],
        out_specs=rows,
        out_shape=jax.ShapeDtypeStruct((n, D_MODEL), F32),
        compiler_params=_cparams(("parallel",)),
        name="final_norm",
    )(x, delta, mod, g)


def _chunk_masks(mask_scr):
    r = lax.broadcasted_iota(jnp.int32, (PACK, PACK), 0)
    c = lax.broadcasted_iota(jnp.int32, (PACK, PACK), 1)
    same = (r // CHUNK) == (c // CHUNK)
    tr = r % CHUNK
    tc = c % CHUNK
    mask_scr[0] = jnp.where(same & (tc <= tr), 0.0, NEG_BIG)
    mask_scr[1] = jnp.where(same & (tc >= tr), 0.0, NEG_BIG)
    mask_scr[2] = (same & (tc < tr)).astype(F32)
    mask_scr[3] = (same & (tc > tr)).astype(F32)
    for i in range(N_MERGE):
        m = 1 << i
        mask_scr[4 + i] = (((r // (2 * m)) == (c // (2 * m))) & ((r // m) != (c // m))).astype(F32)


def _cum_matrix(d):
    i = lax.broadcasted_iota(jnp.int32, (CHUNK, CHUNK), 0)
    j = lax.broadcasted_iota(jnp.int32, (CHUNK, CHUNK), 1)
    return ((j <= i) if d == 0 else (j >= i)).astype(F32)


def _tri_inverse(a, mask_scr):
    n = a * mask_scr[4]
    for i in range(1, N_MERGE):
        c = a * mask_scr[4 + i]
        x = c + _bdot(c, n)
        yield
        n = n + x + _bdot(n, x)
        yield
    return n


def _interleave(gens):
    results = [None] * len(gens)
    active = list(range(len(gens)))
    while active:
        for i in list(active):
            try:
                next(gens[i])
            except StopIteration as stop:
                results[i] = stop.value
                active.remove(i)
    return results


def _fill_halo(pad_scr, src_ref, t, width):
    zeros = jnp.zeros((HALO, width), F32)
    pad_scr[0:HALO, :] = zeros
    pad_scr[HALO + t:2 * HALO + t, :] = zeros

    def copy(i, carry):
        r0 = pl.multiple_of(i * CHUNK, CHUNK)
        pad_scr[pl.ds(HALO + r0, CHUNK), :] = src_ref[pl.ds(r0, CHUNK), :width]
        return carry

    lax.fori_loop(0, t // CHUNK, copy, 0)


def _dn_kernel(t, has_s0, u_ref, cw_ref, gp_ref, nw_ref, *rest):
    if has_s0:
        s0_ref, o_ref, sfin_ref = rest[:3]
        scr = rest[3:]
    else:
        o_ref, sfin_ref = rest[:2]
        scr = rest[2:]
    pad_scr, q_scr, k_scr, v_scr, bg_scr, acc_scr, s_scr, mask_scr = scr
    n_chunks = t // CHUNK
    qkv_w = 3 * DN_W

    if has_s0:
        s_scr[...] = s0_ref[0]
    else:
        s_scr[...] = jnp.zeros_like(s_scr)
    acc_scr[...] = jnp.zeros_like(acc_scr)
    _chunk_masks(mask_scr)
    _fill_halo(pad_scr, u_ref, t, qkv_w)

    lane = lax.broadcasted_iota(jnp.int32, (1, LANES), 1)
    neg_a = -jnp.exp(gp_ref[0:1, :])
    dt_b = gp_ref[1:2, :]

    def prep(i, carry):
        r0 = pl.multiple_of(i * CHUNK, CHUNK)
        rows = pl.ds(r0, CHUNK)
        for cg in range(qkv_w // LANES):
            cols = slice(cg * LANES, (cg + 1) * LANES)
            win = pad_scr[pl.ds(r0, CHUNK + 2 * HALO), cols]
            acc = None
            for j in range(CONV_K):
                off = HALO - CONV_K // 2 + j
                term = cw_ref[j:j + 1, cols] * win[off:off + CHUNK, :]
                acc = term if acc is None else acc + term
            y = _silu(acc)
            which, hc = divmod(cg, DN_H)
            hcols = slice(hc * LANES, (hc + 1) * LANES)
            if which < 2:
                y = y * lax.rsqrt(jnp.sum(y * y, axis=-1, keepdims=True) + 1e-6)
            if which == 0:
                q_scr[rows, hcols] = y * DN_DK ** -0.5
            elif which == 1:
                k_scr[rows, hcols] = y
            else:
                v_scr[rows, hcols] = y
        ub = u_ref[rows, 4 * DN_W:4 * DN_W + LANES]
        bg_scr[rows, :] = jnp.where(lane < 2 * DN_H, jax.nn.sigmoid(ub), neg_a * _softplus(ub + dt_b))
        return carry

    lax.fori_loop(0, n_chunks, prep, 0)

    def stack(ref, rows):
        return jnp.concatenate([ref[rows, h * LANES:(h + 1) * LANES] for h in range(DN_H)], axis=0)

    def local(d, rows):
        last = CHUNK - 1 if d == 0 else 0
        bg = bg_scr[rows, :]
        g_cum = _hdot(_cum_matrix(d), bg)
        lanes_g = [2 * DN_H + d * DN_H + h for h in range(DN_H)]
        beta = jnp.concatenate([bg[:, d * DN_H + h:d * DN_H + h + 1] for h in range(DN_H)], axis=0)
        g_col = jnp.concatenate([g_cum[:, l:l + 1] for l in lanes_g], axis=0)
        g_tot = jnp.concatenate(
            [jnp.broadcast_to(g_cum[last:last + 1, l:l + 1], (CHUNK, 1)) for l in lanes_g], axis=0)
        g_row = jnp.transpose(jnp.broadcast_to(g_col, (PACK, LANES)))[0:1, :]
        decay = jnp.exp(g_col - g_row + mask_scr[d])
        qc = stack(q_scr, rows)
        kc = stack(k_scr, rows)
        vc = stack(v_scr, rows)
        qkk = _bdot_nt(jnp.concatenate([qc, kc], axis=0), kc)
        yield
        qk = (qkk[:PACK] * decay).astype(BF16)
        low = (beta * qkk[PACK:]) * (decay * mask_scr[2 + d])
        nm = yield from _tri_inverse(-low, mask_scr)
        eg = jnp.exp(g_col)
        rhs = jnp.concatenate([beta * vc, (beta * eg) * kc], axis=1)
        sol = rhs + _bdot(nm, rhs)
        yield
        wq = [jnp.concatenate([sol[h * CHUNK:(h + 1) * CHUNK, DN_DV:],
                               (qc * eg)[h * CHUNK:(h + 1) * CHUNK]], axis=0).astype(BF16)
              for h in range(DN_H)]
        k_g = (kc * jnp.exp(g_tot - g_col)).astype(BF16)
        return sol[:, :DN_DV], wq, qk, k_g, jnp.exp(g_tot)

    def carried(d, chunk_rows, chunk_parts):
        for rows, parts in zip(chunk_rows, chunk_parts):
            u_t, wq, qk, k_g, e_tot = parts
            us = []
            oq = []
            for h in range(DN_H):
                hr = slice(h * CHUNK, (h + 1) * CHUNK)
                ws = _bdot(wq[h], s_scr[d * DN_H + h])
                us.append(u_t[hr] - ws[:CHUNK])
                oq.append(ws[CHUNK:])
            yield
            u_all = jnp.concatenate(us, axis=0).astype(BF16)
            o_all = jnp.concatenate(oq, axis=0) + _bdot(qk, u_all)
            for h in range(DN_H):
                hr = slice(h * CHUNK, (h + 1) * CHUNK)
                idx = d * DN_H + h
                s_scr[idx] = e_tot[h * CHUNK:h * CHUNK + 1, :] * s_scr[idx] + _bdot_tn(k_g[hr], u_all[hr])
                acc_scr[rows, h * LANES:(h + 1) * LANES] += o_all[hr]
            yield

    def body(n, carry):
        rows = [[], []]
        for j in range(CHUNKS_PER_STEP):
            for d in range(2):
                ch = n * CHUNKS_PER_STEP + j
                ch = ch if d == 0 else n_chunks - 1 - ch
                rows[d].append(pl.ds(pl.multiple_of(ch * CHUNK, CHUNK), CHUNK))
        flat = [(d, r) for d in range(2) for r in rows[d]]
        parts = _interleave([local(d, r) for d, r in flat])
        by_dir = [[p for (dd, _), p in zip(flat, parts) if dd == d] for d in range(2)]
        _interleave([carried(d, rows[d], by_dir[d]) for d in range(2)])
        return carry

    lax.fori_loop(0, n_chunks // CHUNKS_PER_STEP, body, 0)
    sfin_ref[0] = s_scr[...]

    def post(i, carry):
        rows = pl.ds(pl.multiple_of(i * CHUNK, CHUNK), CHUNK)
        for h in range(DN_H):
            cols = slice(h * LANES, (h + 1) * LANES)
            o = acc_scr[rows, cols]
            on = o * lax.rsqrt(jnp.mean(o * o, axis=-1, keepdims=True) + RMS_EPS) * nw_ref[...]
            z = u_ref[rows, 3 * DN_W + h * LANES:3 * DN_W + (h + 1) * LANES]
            o_ref[rows, cols] = on * _silu(z)
        return carry

    lax.fori_loop(0, n_chunks, post, 0)


def deltanet_mixer(u_dn, row0, b, t, conv_w, a_log, dt_bias, norm_w, s0):
    nh2 = 2 * DN_H
    blk0 = row0 // t
    cw = jnp.pad(conv_w, ((0, SUBLANES - CONV_K), (0, 0)))
    gp = jnp.zeros((SUBLANES, LANES), F32)
    gp = gp.at[0, nh2:2 * nh2].set(a_log.reshape(-1)).at[1, nh2:2 * nh2].set(dt_bias.reshape(-1))
    st = pl.BlockSpec((1, nh2, DN_DK, DN_DV), lambda i: (i, 0, 0, 0))
    in_specs = [pl.BlockSpec((t, DN_PAD), lambda i: (blk0 + i, 0)),
                pl.BlockSpec((SUBLANES, 3 * DN_W), lambda i: (0, 0)),
                pl.BlockSpec((SUBLANES, LANES), lambda i: (0, 0)),
                pl.BlockSpec((1, DN_DV), lambda i: (0, 0))]
    args = [u_dn, cw, gp, norm_w[None, :]]
    if s0 is not None:
        in_specs.append(st)
        args.append(s0.reshape(b, nh2, DN_DK, DN_DV))
    o, sfin = pl.pallas_call(
        functools.partial(_dn_kernel, t, s0 is not None),
        grid=(b,),
        in_specs=in_specs,
        out_specs=[pl.BlockSpec((t, DN_W), lambda i: (i, 0)), st],
        out_shape=[jax.ShapeDtypeStruct((b * t, DN_W), F32),
                   jax.ShapeDtypeStruct((b, nh2, DN_DK, DN_DV), F32)],
        scratch_shapes=[pltpu.VMEM((t + 2 * HALO, 3 * DN_W), F32),
                        pltpu.VMEM((t, DN_W), F32), pltpu.VMEM((t, DN_W), F32), pltpu.VMEM((t, DN_W), F32),
                        pltpu.VMEM((t, LANES), F32), pltpu.VMEM((t, DN_W), F32),
                        pltpu.VMEM((nh2, DN_DK, DN_DV), F32),
                        pltpu.VMEM((N_MASKS, PACK, PACK), F32)],
        compiler_params=_cparams(("parallel",)),
        name="deltanet",
    )(*args)
    return o, sfin.reshape(b, 2, DN_H, DN_DK, DN_DV)


def _rw_kernel(t, has_s0, u_ref, mu_ref, w2_ref, a2_ref, g2_ref, wa0_ref, pv_ref, *rest):
    if has_s0:
        s0_ref, o_ref, sfin_ref = rest[:3]
        scr = rest[3:]
    else:
        o_ref, sfin_ref = rest[:2]
        scr = rest[2:]
    pad_scr, r_scr, v_scr, kk_scr, gate_scr, bonus_scr, lw_scr, kd_scr, b_scr, y_scr, z_scr, mask_scr = scr
    n_chunks = t // CHUNK
    w = RW_W

    if has_s0:
        z_scr[...] = s0_ref[0]
    else:
        z_scr[...] = jnp.zeros_like(z_scr)
    y_scr[...] = jnp.zeros_like(y_scr)
    _chunk_masks(mask_scr)
    _fill_halo(pad_scr, u_ref, t, RW_COLS)

    low_half = lax.broadcasted_iota(jnp.int32, (1, LANES), 1) < RW_DH

    def seg_sum(x):
        outs = []
        for gi in range(w // LANES):
            xg = x[:, gi * LANES:(gi + 1) * LANES]
            lo = jnp.sum(jnp.where(low_half, xg, 0.0), axis=-1, keepdims=True)
            hi = jnp.sum(jnp.where(low_half, 0.0, xg), axis=-1, keepdims=True)
            outs.append(jnp.where(low_half, lo, hi))
        return jnp.concatenate(outs, axis=1)

    head_mask = lax.broadcasted_iota(jnp.int32, (PACK, w), 0) // CHUNK == \
        lax.broadcasted_iota(jnp.int32, (PACK, w), 1) // RW_DH

    k_k = pv_ref[0:1, :]
    k_a = pv_ref[1:2, :]
    r_k = pv_ref[2:3, :]
    ln_w = pv_ref[3:4, :]
    ln_b = pv_ref[4:5, :]

    def prep(i, carry):
        r0 = pl.multiple_of(i * CHUNK, CHUNK)
        rows = pl.ds(r0, CHUNK)
        xs = []
        for cg in range(RW_COLS // LANES):
            cols = slice(cg * LANES, (cg + 1) * LANES)
            win = pad_scr[pl.ds(r0, CHUNK + 2 * HALO), cols]
            cur = win[HALO:HALO + CHUNK, :]
            nb = 0.5 * (win[HALO - 1:HALO - 1 + CHUNK, :] + win[HALO + 1:HALO + 1 + CHUNK, :])
            xs.append(cur + mu_ref[:, cols] * (nb - cur))
        x = jnp.concatenate(xs, axis=1)
        r = x[:, :w]
        k = x[:, w:2 * w]
        v = x[:, 2 * w:3 * w]
        wd = x[:, 3 * w:3 * w + 2 * RW_W_LORA]
        ad = x[:, 3 * w + 2 * RW_W_LORA:3 * w + 2 * RW_W_LORA + 2 * RW_A_LORA]
        gd = x[:, 3 * w + 2 * RW_W_LORA + 2 * RW_A_LORA:]
        w_log = -_softplus(-(wa0_ref[0:1, :] + _bdot(jnp.tanh(wd), w2_ref[...]))) - 0.5
        al = jax.nn.sigmoid(wa0_ref[1:2, :] + _bdot(ad, a2_ref[...]))
        kk = k * k_k
        kk = kk * lax.rsqrt(seg_sum(kk * kk) + 1e-6)
        r_scr[rows, :] = r
        v_scr[rows, :] = v
        kk_scr[rows, :] = kk
        gate_scr[rows, :] = _bdot(jax.nn.sigmoid(gd), g2_ref[...])
        bonus_scr[rows, :] = seg_sum(r * k * r_k) * v
        lw_scr[rows, :] = -jnp.exp(w_log)
        for d in range(2):
            dc = slice(d * w, (d + 1) * w)
            kd_scr[rows, dc] = k * (1.0 + (al[:, dc] - 1.0) * k_a)
            b_scr[rows, dc] = kk * al[:, dc]
        return carry

    lax.fori_loop(0, n_chunks, prep, 0)

    def stack(x):
        return jnp.where(head_mask, jnp.concatenate([x] * RW_H, axis=0), 0.0)

    def local(d, rows):
        dc = slice(d * w, (d + 1) * w)
        last = CHUNK - 1 if d == 0 else 0
        lw = lw_scr[rows, dc]
        lc = _hdot(_cum_matrix(d), lw)
        l_tot = lc[last:last + 1, :]
        e_in = jnp.exp(lc)
        e_ex = jnp.exp(lc - lw)
        e_neg = jnp.exp(-lc)
        e_rem = jnp.exp(l_tot - lc)
        kk = kk_scr[rows, :]
        kd = kd_scr[rows, dc]
        bb = b_scr[rows, dc]
        a_t = stack(-kk * e_ex)
        r_t = stack(r_scr[rows, :] * e_in).astype(BF16)
        b_t = stack(bb * e_neg).astype(BF16)
        k_t = stack(kd * e_neg).astype(BF16)
        kb_h = jnp.concatenate([stack(kd * e_rem), stack(bb * e_rem)], axis=0).astype(BF16)
        v_m = stack(v_scr[rows, :]).astype(BF16)
        g = _bdot_nt(jnp.concatenate([a_t.astype(BF16), r_t], axis=0),
                     jnp.concatenate([b_t, k_t], axis=0))
        yield
        strict = mask_scr[2 + d]
        incl = jnp.where(mask_scr[d] == 0.0, 1.0, 0.0)
        a_ab = g[:PACK, :PACK] * strict
        a_k = jnp.concatenate([g[:PACK, PACK:] * strict, g[PACK:, PACK:] * incl], axis=0).astype(BF16)
        a_rb = (g[PACK:, :PACK] * incl).astype(BF16)
        nm = yield from _tri_inverse(a_ab, mask_scr)
        av = _bdot(a_k, v_m)
        yield
        rhs = jnp.concatenate([av[:PACK], a_t], axis=1)
        sol = rhs + _bdot(nm, rhs)
        yield
        ar = jnp.concatenate([sol[:, w:].astype(BF16), r_t], axis=0)
        return sol[:, :w], ar, av[PACK:], a_rb, v_m, kb_h, jnp.exp(l_tot)

    def carried(d, chunk_rows, chunk_parts):
        for rows, parts in zip(chunk_rows, chunk_parts):
            p0, ar, y0, a_rb, v_m, kb_h, e_tot = parts
            zt = z_scr[d]
            az = _bdot_nt(ar, zt)
            yield
            p = (p0 + az[:PACK]).astype(BF16)
            y = y0 + az[PACK:] + _bdot(a_rb, p)
            y_scr[rows, :] += (y[0:CHUNK] + y[CHUNK:2 * CHUNK]) + (y[2 * CHUNK:3 * CHUNK] + y[3 * CHUNK:])
            yield
            z_scr[d] = zt * e_tot + _bdot_tn(jnp.concatenate([v_m, p], axis=0), kb_h)
            yield

    def body(n, carry):
        rows = [[], []]
        for j in range(CHUNKS_PER_STEP):
            for d in range(2):
                ch = n * CHUNKS_PER_STEP + j
                ch = ch if d == 0 else n_chunks - 1 - ch
                rows[d].append(pl.ds(pl.multiple_of(ch * CHUNK, CHUNK), CHUNK))
        flat = [(d, r) for d in range(2) for r in rows[d]]
        parts = _interleave([local(d, r) for d, r in flat])
        by_dir = [[p for (dd, _), p in zip(flat, parts) if dd == d] for d in range(2)]
        _interleave([carried(d, rows[d], by_dir[d]) for d in range(2)])
        return carry

    lax.fori_loop(0, n_chunks // CHUNKS_PER_STEP, body, 0)
    sfin_ref[0] = z_scr[...]

    def post(i, carry):
        rows = pl.ds(pl.multiple_of(i * CHUNK, CHUNK), CHUNK)
        y = y_scr[rows, :]
        m = seg_sum(y) * (1.0 / RW_DH)
        yc = y - m
        var = seg_sum(yc * yc) * (1.0 / RW_DH)
        yn = yc * lax.rsqrt(var + RW_GN_EPS) * ln_w + ln_b
        o_ref[rows, :] = (yn + bonus_scr[rows, :]) * gate_scr[rows, :]
        return carry

    lax.fori_loop(0, n_chunks, post, 0)


def _dir_blocks(m):
    z = jnp.zeros_like(m[0])
    return jnp.concatenate([jnp.concatenate([m[0], z], axis=1), jnp.concatenate([z, m[1]], axis=1)], axis=0)


def rwkv7_mixer(u_rw, row0, b, t, mu, w0, w2, a0, a2, g2, k_k, k_a, r_k, ln_w, ln_b, s0):
    w = RW_W
    blk0 = row0 // t
    w2b = _dir_blocks(w2).astype(BF16)
    a2b = _dir_blocks(a2).astype(BF16)
    wa0 = jnp.zeros((SUBLANES, 2 * w), F32).at[0].set(w0.reshape(-1)).at[1].set(a0.reshape(-1))
    pv = jnp.zeros((SUBLANES, w), F32)
    for i, p in enumerate((k_k, k_a, r_k.reshape(-1), ln_w, ln_b)):
        pv = pv.at[i].set(p)
    full = lambda a: pl.BlockSpec(a.shape, lambda i: (0,) * a.ndim)
    st = pl.BlockSpec((1, 2, w, w), lambda i: (i, 0, 0, 0))
    args = [u_rw, mu[None, :], w2b, a2b, g2.astype(BF16), wa0, pv]
    in_specs = [pl.BlockSpec((t, RW_COLS), lambda i: (blk0 + i, 0))] + [full(a) for a in args[1:]]
    eye_h = jnp.eye(RW_H, dtype=F32)
    if s0 is not None:
        in_specs.append(st)
        args.append(jnp.einsum('bdhvk,hg->bdhvgk', s0, eye_h).reshape(b, 2, w, w))
    seq = lambda width: pltpu.VMEM((t, width), F32)
    o, sfin = pl.pallas_call(
        functools.partial(_rw_kernel, t, s0 is not None),
        grid=(b,),
        in_specs=in_specs,
        out_specs=[pl.BlockSpec((t, w), lambda i: (i, 0)), st],
        out_shape=[jax.ShapeDtypeStruct((b * t, w), F32), jax.ShapeDtypeStruct((b, 2, w, w), F32)],
        scratch_shapes=[pltpu.VMEM((t + 2 * HALO, RW_COLS), F32),
                        seq(w), seq(w), seq(w), seq(w), seq(w), seq(2 * w), seq(2 * w), seq(2 * w), seq(w),
                        pltpu.VMEM((2, w, w), F32), pltpu.VMEM((N_MASKS, PACK, PACK), F32)],
        compiler_params=_cparams(("parallel",)),
        name="rwkv7",
    )(*args)
    sf = sfin.reshape(b, 2, RW_H, RW_DH, RW_H, RW_DH)
    return o, jnp.einsum('bdhvhk->bdhvk', sf)


def _gla_kernel(t, has_s0, u_ref, gk2_ref, gkb_ref, nw_ref, *rest):
    if has_s0:
        s0_ref, o_ref, sfin_ref, s_scr, gk_scr, acc_scr = rest
        s_scr[...] = s0_ref[0]
    else:
        o_ref, sfin_ref, s_scr, gk_scr, acc_scr = rest
        s_scr[...] = jnp.zeros_like(s_scr)
    c = GLA_CHUNK
    n_chunks = t // c
    acc_scr[...] = jnp.zeros_like(acc_scr)
    q_cols = slice(0, GLA_KW)
    k_cols = slice(GLA_KW, 2 * GLA_KW)
    v_cols = slice(2 * GLA_KW, 2 * GLA_KW + GLA_VW)
    g_off = 2 * GLA_KW + GLA_VW
    lora_cols = slice(g_off + GLA_VW, g_off + GLA_VW + LANES)

    def prep(i, carry):
        rows = pl.ds(pl.multiple_of(i * CHUNK, CHUNK), CHUNK)
        z = _bdot(u_ref[rows, lora_cols], gk2_ref[...]) + gkb_ref[...]
        gk_scr[rows, :] = -_softplus(-z) * (1.0 / GLA_GATE_NORM)
        return carry

    lax.fori_loop(0, t // CHUNK, prep, 0)

    ii = lax.broadcasted_iota(jnp.int32, (c, c), 0)
    jj = lax.broadcasted_iota(jnp.int32, (c, c), 1)
    cum_m = ((ii >= jj).astype(F32), (ii <= jj).astype(F32))
    row_i = lax.broadcasted_iota(jnp.int32, (c, 1), 0)
    hk = lax.broadcasted_iota(jnp.int32, (GLA_KW, GLA_VW), 0) // GLA_DK
    hv = lax.broadcasted_iota(jnp.int32, (GLA_KW, GLA_VW), 1) // GLA_DV
    head_sum = (hk == hv).astype(BF16)
    bd_mask = (lax.broadcasted_iota(jnp.int32, (GLA_VW, GLA_KW), 0) // GLA_DV
               == lax.broadcasted_iota(jnp.int32, (GLA_VW, GLA_KW), 1) // GLA_DK)

    def local(d, rows):
        qc = u_ref[rows, q_cols] * GLA_DK ** -0.5
        kc = u_ref[rows, k_cols]
        vc = u_ref[rows, v_cols]
        g = gk_scr[rows, d * GLA_KW:(d + 1) * GLA_KW]
        bc = _hdot(cum_m[d], g)
        yield
        last = c - 1 if d == 0 else 0
        bc_last = bc[last:last + 1, :]
        terms = []
        for j in range(c):
            keep = (row_i >= j) if d == 0 else (row_i <= j)
            dec = jnp.exp(jnp.where(keep, bc - bc[j:j + 1, :], NEG_BIG))
            terms.append(qc * kc[j:j + 1, :] * dec)
        t_all = jnp.concatenate(terms, axis=0).astype(BF16)
        att = jnp.dot(t_all, head_sum, preferred_element_type=F32)
        yield
        o = None
        for j in range(c):
            term = att[j * c:(j + 1) * c, :] * vc[j:j + 1, :]
            o = term if o is None else o + term
        q_g = (qc * jnp.exp(bc)).astype(BF16)
        upd = jnp.where(bd_mask, _bdot_tn(vc, kc * jnp.exp(bc_last - bc)), 0.0)
        yield
        return o, q_g, upd, jnp.exp(bc_last)

    def carried(d, chunk_rows, chunk_parts):
        for rows, parts in zip(chunk_rows, chunk_parts):
            o, q_g, upd, e_last = parts
            st = s_scr[d]
            acc_scr[rows, :] += o + _bdot_nt(q_g, st)
            s_scr[d] = st * e_last + upd
            yield

    def body(n, carry):
        rows = [[], []]
        for j in range(GLA_CHUNKS_PER_STEP):
            for d in range(2):
                ch = n * GLA_CHUNKS_PER_STEP + j
                ch = ch if d == 0 else n_chunks - 1 - ch
                rows[d].append(pl.ds(pl.multiple_of(ch * c, c), c))
        flat = [(d, r) for d in range(2) for r in rows[d]]
        parts = _interleave([local(d, r) for d, r in flat])
        by_dir = [[p for (dd, _), p in zip(flat, parts) if dd == d] for d in range(2)]
        _interleave([carried(d, rows[d], by_dir[d]) for d in range(2)])
        return carry

    lax.fori_loop(0, n_chunks // GLA_CHUNKS_PER_STEP, body, 0)
    sfin_ref[0] = s_scr[...]

    low_half = lax.broadcasted_iota(jnp.int32, (1, LANES), 1) < GLA_DV

    def post(i, carry):
        rows = pl.ds(pl.multiple_of(i * CHUNK, CHUNK), CHUNK)
        for gi in range(GLA_VW // LANES):
            cols = slice(gi * LANES, (gi + 1) * LANES)
            o = acc_scr[rows, cols]
            sq = o * o
            lo = jnp.sum(jnp.where(low_half, sq, 0.0), axis=-1, keepdims=True)
            hi = jnp.sum(jnp.where(low_half, 0.0, sq), axis=-1, keepdims=True)
            ms = jnp.where(low_half, lo, hi) * (1.0 / GLA_DV)
            gate = u_ref[rows, g_off + gi * LANES:g_off + (gi + 1) * LANES]
            o_ref[rows, cols] = o * lax.rsqrt(ms + RMS_EPS) * nw_ref[:, cols] * _silu(gate)
        return carry

    lax.fori_loop(0, t // CHUNK, post, 0)


def gla_mixer(u_gla, row0, b, t, gk2, gk_b, norm_w, s0):
    blk0 = row0 // t
    gk2b = jnp.pad(_dir_blocks(gk2), ((0, LANES - 2 * GLA_GK_LORA), (0, 0))).astype(BF16)
    st = pl.BlockSpec((1, 2, GLA_VW, GLA_KW), lambda i: (i, 0, 0, 0))
    in_specs = [pl.BlockSpec((t, GLA_PAD), lambda i: (blk0 + i, 0)),
                pl.BlockSpec((LANES, 2 * GLA_KW), lambda i: (0, 0)),
                pl.BlockSpec((1, 2 * GLA_KW), lambda i: (0, 0)),
                pl.BlockSpec((1, GLA_VW), lambda i: (0, 0))]
    args = [u_gla, gk2b, gk_b.reshape(1, 2 * GLA_KW), jnp.tile(norm_w, GLA_H)[None, :]]
    if s0 is not None:
        eye_h = jnp.eye(GLA_H, dtype=F32)
        in_specs.append(st)
        args.append(jnp.einsum('bdhkv,hg->bdhvgk', s0, eye_h).reshape(b, 2, GLA_VW, GLA_KW))
    o, sfin_t = pl.pallas_call(
        functools.partial(_gla_kernel, t, s0 is not None),
        grid=(b,),
        in_specs=in_specs,
        out_specs=[pl.BlockSpec((t, GLA_VW), lambda i: (i, 0)), st],
        out_shape=[jax.ShapeDtypeStruct((b * t, GLA_VW), F32),
                   jax.ShapeDtypeStruct((b, 2, GLA_VW, GLA_KW), F32)],
        scratch_shapes=[pltpu.VMEM((2, GLA_VW, GLA_KW), F32), pltpu.VMEM((t, 2 * GLA_KW), F32),
                        pltpu.VMEM((t, GLA_VW), F32)],
        compiler_params=_cparams(("parallel",)),
        name="gla",
    )(*args)
    sf = sfin_t.reshape(b, 2, GLA_H, GLA_DV, GLA_H, GLA_DK)
    return o, jnp.einsum('bdhvhk->bdhkv', sf)


def _moe_kernel(layer, be_ref, nxt_ref, slot_ref, rs_ref, nb_ref, xs_hbm, wgu_hbm, bgu_ref, wdn_hbm, bdn_ref,
                y_ref, wgu_f32, wdn_f32, wgu_bf, wdn_bf, xwin, sem, xsem):
    i = pl.program_id(0)
    used = i < nb_ref[0]
    e = be_ref[i]
    prev = be_ref[jnp.maximum(i - 1, 0)]
    fresh = jnp.logical_or(i == 0, e != prev)
    slot = slot_ref[i]

    def weight_copies(expert, s):
        return (pltpu.make_async_copy(wgu_hbm.at[layer, expert], wgu_f32.at[s], sem.at[0, s]),
                pltpu.make_async_copy(wdn_hbm.at[layer, expert], wdn_f32.at[s], sem.at[1, s]))

    def window_copy(blk):
        first = pl.multiple_of(rs_ref[blk] // BF16_ROWS * BF16_ROWS, BF16_ROWS)
        s = blk % 2
        return pltpu.make_async_copy(xs_hbm.at[pl.ds(first, MOE_WIN), :], xwin.at[s], xsem.at[s])

    @pl.when(jnp.logical_and(used, i == 0))
    def _():
        window_copy(i).start()
        for cp in weight_copies(e, slot):
            cp.start()

    @pl.when(i + 1 < nb_ref[0])
    def _():
        window_copy(i + 1).start()

    @pl.when(jnp.logical_and(used, fresh))
    def _():
        for cp in weight_copies(e, slot):
            cp.wait()

        @pl.when(nxt_ref[i] >= 0)
        def _():
            for cp in weight_copies(nxt_ref[i], 1 - slot):
                cp.start()

        wgu_bf[...] = wgu_f32[slot].astype(BF16)
        wdn_bf[...] = wdn_f32[slot].astype(BF16)

    @pl.when(used)
    def _():
        window_copy(i).wait()
        shift = rs_ref[i] % BF16_ROWS
        r = lax.broadcasted_iota(jnp.int32, (MOE_BM, MOE_WIN), 0)
        c = lax.broadcasted_iota(jnp.int32, (MOE_BM, MOE_WIN), 1)
        pick = (c == r + shift).astype(BF16)
        x = jnp.dot(pick, xwin[i % 2], preferred_element_type=F32).astype(BF16)
        gu = jnp.dot(x, wgu_bf[...], preferred_element_type=F32) + bgu_ref[0, 0]
        gt = jnp.minimum(gu[:, :D_FF], SWIGLU_LIMIT)
        up = jnp.clip(gu[:, D_FF:], -SWIGLU_LIMIT, SWIGLU_LIMIT)
        act = (up + 1.0) * (gt * jax.nn.sigmoid(gt * SWIGLU_ALPHA))
        y = jnp.dot(act.astype(BF16), wdn_bf[...], preferred_element_type=F32) + bdn_ref[0, 0]
        y_ref[...] = y.astype(y_ref.dtype)

    @pl.when(jnp.logical_not(used))
    def _():
        y_ref[...] = jnp.zeros_like(y_ref)


def moe_experts(layer, block_e, next_e, slot, row_start, n_used, xs, w_gu, b_gu, w_dn, b_dn):
    n_blocks = block_e.shape[0]
    depth = w_gu.shape[0]
    bias = lambda width: pl.BlockSpec((1, 1, 1, width), lambda i, be, nx, sl, rs, nb: (layer, be[i], 0, 0))
    grid_spec = pltpu.PrefetchScalarGridSpec(
        num_scalar_prefetch=5,
        grid=(n_blocks,),
        in_specs=[
            pl.BlockSpec(memory_space=pl.ANY),
            pl.BlockSpec(memory_space=pl.ANY),
            bias(2 * D_FF),
            pl.BlockSpec(memory_space=pl.ANY),
            bias(D_MODEL),
        ],
        out_specs=pl.BlockSpec((MOE_BM, D_MODEL), lambda i, be, nx, sl, rs, nb: (i, 0)),
        scratch_shapes=[pltpu.VMEM((2, D_MODEL, 2 * D_FF), F32), pltpu.VMEM((2, D_FF, D_MODEL), F32),
                        pltpu.VMEM((D_MODEL, 2 * D_FF), BF16), pltpu.VMEM((D_FF, D_MODEL), BF16),
                        pltpu.VMEM((2, MOE_WIN, D_MODEL), BF16),
                        pltpu.SemaphoreType.DMA((2, 2)), pltpu.SemaphoreType.DMA((2,))],
    )
    return pl.pallas_call(
        functools.partial(_moe_kernel, layer),
        grid_spec=grid_spec,
        out_shape=jax.ShapeDtypeStruct((n_blocks * MOE_BM, D_MODEL), BF16),
        compiler_params=_cparams(("arbitrary",)),
        name="moe_experts",
    )(block_e, next_e, slot, row_start, n_used, xs, w_gu, b_gu.reshape(depth, N_EXPERTS, 1, 2 * D_FF),
      w_dn, b_dn.reshape(depth, N_EXPERTS, 1, D_MODEL))


def moe_ffn(layer, h2, top_idx, gates, w_gu, b_gu, w_dn, b_dn):
    n_tok = h2.shape[0]
    n_assign = n_tok * TOP_K
    n_blocks = n_assign // MOE_BM + N_EXPERTS
    e_flat = top_idx.reshape(-1).astype(jnp.int32)
    ar = jnp.arange(n_assign, dtype=jnp.int32)
    e_sorted, order = lax.sort((e_flat, ar), num_keys=1, is_stable=True)
    counts = jnp.sum(e_flat[:, None] == jnp.arange(N_EXPERTS, dtype=jnp.int32)[None, :], axis=0, dtype=jnp.int32)
    padded = (counts + MOE_BM - 1) // MOE_BM * MOE_BM
    pad_end = jnp.cumsum(padded)
    pad_start = pad_end - padded
    start = jnp.cumsum(counts) - counts
    dest_sorted = ar + (pad_start - start)[e_sorted]
    _, pos = lax.sort((order, dest_sorted), num_keys=1)
    blk_start = jnp.arange(n_blocks, dtype=jnp.int32) * MOE_BM
    n_used = (pad_end[-1] // MOE_BM).astype(jnp.int32)
    block_e = jnp.sum(pad_end[None, :] <= blk_start[:, None], axis=1, dtype=jnp.int32)
    block_e = jnp.minimum(block_e, N_EXPERTS - 1)
    block_e = jnp.where(jnp.arange(n_blocks) < n_used, block_e, block_e[jnp.maximum(n_used - 1, 0)])
    xs = h2[jnp.concatenate([order // TOP_K, jnp.zeros((MOE_WIN,), jnp.int32)])]
    row_start = jnp.clip(start[block_e] + blk_start - pad_start[block_e], 0, n_assign - 1)
    e_ids = jnp.arange(N_EXPERTS, dtype=jnp.int32)
    owner = jnp.where(counts > 0, e_ids, N_EXPERTS)
    later = jnp.concatenate([lax.cummin(owner, reverse=True)[1:], jnp.full((1,), N_EXPERTS, jnp.int32)])
    next_of = jnp.where(later < N_EXPERTS, later, -1)
    parity = (jnp.cumsum((counts > 0).astype(jnp.int32)) - 1) % 2
    y_buf = moe_experts(layer, block_e, next_of[block_e], parity[block_e].astype(jnp.int32), row_start,
                        n_used.reshape(1), xs, w_gu, b_gu, w_dn, b_dn)
    pos_k = pos.reshape(n_tok, TOP_K)
    out = gates[:, 0:1] * y_buf[pos_k[:, 0]].astype(F32)
    for j in range(1, TOP_K):
        out = out + gates[:, j:j + 1] * y_buf[pos_k[:, j]].astype(F32)
    return out


def grid_pos_embed(rows, dim):
    t = jnp.arange(rows * GRID_W)
    r = (t // GRID_W).astype(F32)
    col = (t % GRID_W).astype(F32)
    quarter = dim // 4
    omega = 1.0 / (POS_THETA ** (jnp.arange(quarter, dtype=F32) / quarter))
    er = r[:, None] * omega
    ec = col[:, None] * omega
    return jnp.concatenate([jnp.sin(er), jnp.cos(er), jnp.sin(ec), jnp.cos(ec)], axis=-1)


def _pad_cols(w_in):
    z = lambda n: jnp.zeros(w_in.shape[:-1] + (n,), w_in.dtype)
    return jnp.concatenate([
        w_in[..., :DN_COLS], z(DN_PAD - DN_COLS),
        w_in[..., DN_COLS:DN_COLS + RW_COLS],
        w_in[..., DN_COLS + RW_COLS:], z(GLA_PAD - GLA_COLS)], axis=-1)


def kernel(x_prompt, x_sample, state_delta, state_rwkv, state_gla, c, c_ctx, w_mod, b_mod, norm_mix, norm_ffn, norm_out, w_in, w_out, dn_conv, dn_a_log, dn_dt_bias, dn_norm, rw_mu, rw_w0, rw_w2, rw_a0, rw_a2, rw_g2, rw_k_k, rw_k_a, rw_r_k, rw_ln_w, rw_ln_b, gla_gk2, gla_gk_b, gla_norm, router_w, router_b, moe_w_gu, moe_b_gu, moe_w_dn, moe_b_dn):
    bp, tp, d = x_prompt.shape
    bs, ts, _ = x_sample.shape
    depth = w_in.shape[0]
    n_ctx = bp * tp
    n_lat = bs * ts
    tiles = dict(n_ctx_tiles=n_ctx // ROW_TILE, tiles_per_latent=ts // ROW_TILE)

    cond = jnp.concatenate([c_ctx[None, :], c, jnp.zeros((SUBLANES - 1 - bs, d), F32)], axis=0)
    mod_all = modulation(cond, w_mod, b_mod).reshape(depth, SUBLANES, N_MOD, d)
    mod_all = jnp.pad(mod_all, ((0, 0), (0, 0), (0, SUBLANES - N_MOD), (0, 0)))

    w_in_p = _pad_cols(w_in).astype(BF16)
    w_out_b = w_out.astype(BF16)
    router_w_p = jnp.pad(router_w, ((0, 0), (0, 0), (0, LANES - N_EXPERTS)))
    router_b_p = jnp.pad(router_b, ((0, 0), (0, LANES - N_EXPERTS)))[:, None, :]

    pos = grid_pos_embed(ts // GRID_W, d)
    x = jnp.concatenate([x_prompt.reshape(n_ctx, d), x_sample.reshape(n_lat, d)], axis=0)
    delta = jnp.concatenate([jnp.zeros((n_ctx, d), F32), jnp.tile(pos, (bs, 1))], axis=0)

    st_dn, st_rw, st_gla = [], [], []
    for l in range(depth):
        mod = mod_all[l]
        x, u_dn, u_rw, u_gla = inproj(x, delta, mod_all[max(l - 1, 0)], mod, norm_mix[l][None, :], w_in_p[l],
                                      gate_row=None if l == 0 else 5, **tiles)
        o_dn, o_rw, o_gla = [], [], []
        for (lo, b, t, s_dn, s_rw, s_gla) in (
                (0, bp, tp, None, None, None),
                (n_ctx, bs, ts, state_delta[:, l], state_rwkv[:, l], state_gla[:, l])):
            od, f_dn = deltanet_mixer(u_dn, lo, b, t, dn_conv[l], dn_a_log[l], dn_dt_bias[l], dn_norm[l], s_dn)
            orw, f_rw = rwkv7_mixer(u_rw, lo, b, t, rw_mu[l], rw_w0[l], rw_w2[l], rw_a0[l], rw_a2[l], rw_g2[l],
                                    rw_k_k[l], rw_k_a[l], rw_r_k[l], rw_ln_w[l], rw_ln_b[l], s_rw)
            og, f_gla = gla_mixer(u_gla, lo, b, t, gla_gk2[l], gla_gk_b[l], gla_norm[l], s_gla)
            o_dn.append(od)
            o_rw.append(orw)
            o_gla.append(og)
            if lo == 0:
                st_dn.append(f_dn)
                st_rw.append(f_rw)
                st_gla.append(f_gla)
        x, h2, top_idx, gates = outproj(jnp.concatenate(o_dn, axis=0), jnp.concatenate(o_rw, axis=0),
                                        jnp.concatenate(o_gla, axis=0), x, mod, norm_ffn[l][None, :],
                                        w_out_b[l], router_w_p[l], router_b_p[l], **tiles)
        delta = moe_ffn(l, h2, top_idx[:, :TOP_K], gates[:, :TOP_K], moe_w_gu, moe_b_gu, moe_w_dn, moe_b_dn)
    y = final_norm(x, delta, mod_all[depth - 1], norm_out[None, :], **tiles)
    y_prompt = y[:n_ctx].reshape(bp, tp, d)
    y_sample = y[n_ctx:].reshape(bs, ts, d)
    return (y_prompt, y_sample, jnp.stack(st_dn, axis=1), jnp.stack(st_rw, axis=1), jnp.stack(st_gla, axis=1))
```
